```python
import math
import jax, jax.numpy as jnp
from jax import lax
import numpy as np

D_MODEL = 1024
BATCH = 8
SEQ = 4096
DEPTH = 2

N_MIXERS = 2
CHUNK = 64
NORM_EPS = 1e-6
L2_EPS = 1e-6

GLA_HEADS = 4
GLA_KEY = D_MODEL // 2
GLA_VAL = D_MODEL
GLA_DK = GLA_KEY // GLA_HEADS
GLA_DV = GLA_VAL // GLA_HEADS
GLA_RANK = 16
GLA_TAU = 16.0
GLA_IN = 2 * GLA_KEY + 2 * GLA_VAL + GLA_RANK

GDN_HEAD_DIM = 128
GDN_QK_HEADS = D_MODEL // GDN_HEAD_DIM
GDN_V_HEADS = 2 * GDN_QK_HEADS
GDN_KEY = GDN_QK_HEADS * GDN_HEAD_DIM
GDN_VAL = GDN_V_HEADS * GDN_HEAD_DIM
GDN_CONV = 4
GDN_CONV_CH = 2 * GDN_KEY + GDN_VAL
GDN_IN = GDN_CONV_CH + GDN_VAL + 2 * GDN_V_HEADS

D_FF = -(-8 * D_MODEL // (3 * 256)) * 256

kernel_name = 'gla_gated_deltanet_interleaved_hybrid'


def rms_norm(x, w):
    xf = x.astype(jnp.float32)
    y = xf * lax.rsqrt(jnp.mean(xf * xf, axis=-1, keepdims=True) + NORM_EPS)
    return (y * w.astype(jnp.float32)).astype(x.dtype)


def l2_norm(x):
    xf = x.astype(jnp.float32)
    return xf * lax.rsqrt(jnp.sum(xf * xf, axis=-1, keepdims=True) + L2_EPS)


def to_chunks(t):
    b, l, h, d = t.shape
    return t.reshape(b, l // CHUNK, CHUNK, h, d).transpose(1, 0, 3, 2, 4)


def from_chunks(t):
    n, b, h, c, d = t.shape
    return t.transpose(1, 0, 3, 2, 4).reshape(b, n * c, h, d)


def causal_depthwise_conv(x, w):
    k = w.shape[0]
    return lax.conv_general_dilated(
        x, w[:, None, :], window_strides=(1,), padding=[(k - 1, 0)],
        dimension_numbers=('NWC', 'WIO', 'NWC'), feature_group_count=x.shape[-1])


def chunked_gla(q, k, v, log_a):
    causal = jnp.tril(jnp.ones((CHUNK, CHUNK), bool))[:, :, None]
    _, b, h, _, dk = q.shape
    dv = v.shape[-1]

    def step(S, inp):
        qc, kc, vc, gc = inp
        G = jnp.cumsum(gc, axis=-2)
        diff = G[..., :, None, :] - G[..., None, :, :]
        decay = jnp.exp(jnp.where(causal, diff, -jnp.inf))
        scores = jnp.einsum('bhid,bhjd,bhijd->bhij', qc, kc, decay)
        o = scores @ vc + jnp.einsum('bhid,bhde->bhie', qc * jnp.exp(G), S)
        G_last = G[..., -1:, :]
        k_dec = kc * jnp.exp(G_last - G)
        S = jnp.exp(G_last)[..., 0, :, None] * S + jnp.einsum('bhjd,bhje->bhde', k_dec, vc)
        return S, o

    S0 = jnp.zeros((b, h, dk, dv), jnp.float32)
    _, o = lax.scan(step, S0, (q, k, v, log_a))
    return o


def chunked_gated_delta(q, k, v, g, beta):
    causal = jnp.tril(jnp.ones((CHUNK, CHUNK), bool))
    strict = jnp.tril(jnp.ones((CHUNK, CHUNK), bool), -1)
    eye = jnp.eye(CHUNK, dtype=jnp.float32)
    _, b, h, _, dk = q.shape
    dv = v.shape[-1]

    def step(S, inp):
        qc, kc, vc, gc, bc = inp
        G = jnp.cumsum(gc, axis=-1)
        decay = jnp.exp(jnp.where(causal, G[..., :, None] - G[..., None, :], -jnp.inf))
        kb = kc * bc[..., None]
        A = jnp.where(strict, jnp.einsum('bhid,bhjd->bhij', kb, kc) * decay, 0.0)
        T = lax.linalg.triangular_solve(eye + A, jnp.broadcast_to(eye, A.shape),
                                        left_side=True, lower=True)
        u = T @ (vc * bc[..., None])
        w = T @ (kb * jnp.exp(G)[..., None])
        v_new = u - w @ S
        attn = jnp.where(causal, jnp.einsum('bhid,bhjd->bhij', qc, kc) * decay, 0.0)
        o = jnp.einsum('bhid,bhde->bhie', qc * jnp.exp(G)[..., None], S) + attn @ v_new
        G_last = G[..., -1]
        k_dec = kc * jnp.exp(G_last[..., None] - G)[..., None]
        S = jnp.exp(G_last)[..., None, None] * S + jnp.einsum('bhjd,bhje->bhde', k_dec, v_new)
        return S, o

    S0 = jnp.zeros((b, h, dk, dv), jnp.float32)
    _, o = lax.scan(step, S0, (q, k, v, g, beta))
    return o


def gla_mixer(h, w_in, w_gate_up, b_gate, norm_w, w_out):
    bsz, l, _ = h.shape
    proj = h @ w_in
    q, k, v, r, g_low = jnp.split(
        proj, [GLA_KEY, 2 * GLA_KEY, 2 * GLA_KEY + GLA_VAL, 2 * GLA_KEY + 2 * GLA_VAL], axis=-1)
    log_a = jax.nn.log_sigmoid((g_low @ w_gate_up + b_gate).astype(jnp.float32)) / GLA_TAU
    qh = to_chunks(q.astype(jnp.float32).reshape(bsz, l, GLA_HEADS, GLA_DK) * GLA_DK ** -0.5)
    kh = to_chunks(k.astype(jnp.float32).reshape(bsz, l, GLA_HEADS, GLA_DK))
    vh = to_chunks(v.astype(jnp.float32).reshape(bsz, l, GLA_HEADS, GLA_DV))
    gh = to_chunks(log_a.reshape(bsz, l, GLA_HEADS, GLA_DK))
    o = from_chunks(chunked_gla(qh, kh, vh, gh))
    o = rms_norm(o, norm_w) * jax.nn.silu(r.astype(jnp.float32).reshape(bsz, l, GLA_HEADS, GLA_DV))
    return o.reshape(bsz, l, GLA_VAL).astype(h.dtype) @ w_out


def gdn_mixer(h, w_in, conv_w, a_log, dt_bias, norm_w, w_out):
    bsz, l, _ = h.shape
    proj = h @ w_in
    qkv, z, b, a = jnp.split(
        proj, [GDN_CONV_CH, GDN_CONV_CH + GDN_VAL, GDN_CONV_CH + GDN_VAL + GDN_V_HEADS], axis=-1)
    qkv = jax.nn.silu(causal_depthwise_conv(qkv, conv_w))
    q, k, v = jnp.split(qkv, [GDN_KEY, 2 * GDN_KEY], axis=-1)
    rep = GDN_V_HEADS // GDN_QK_HEADS
    q = jnp.repeat(l2_norm(q.reshape(bsz, l, GDN_QK_HEADS, GDN_HEAD_DIM)), rep, axis=2)
    k = jnp.repeat(l2_norm(k.reshape(bsz, l, GDN_QK_HEADS, GDN_HEAD_DIM)), rep, axis=2)
    v = v.astype(jnp.float32).reshape(bsz, l, GDN_V_HEADS, GDN_HEAD_DIM)
    beta = jax.nn.sigmoid(b.astype(jnp.float32))
    g = -jnp.exp(a_log.astype(jnp.float32)) * jax.nn.softplus(
        a.astype(jnp.float32) + dt_bias.astype(jnp.float32))
    o = chunked_gated_delta(to_chunks(q * GDN_HEAD_DIM ** -0.5), to_chunks(k), to_chunks(v),
                            to_chunks(g[..., None])[..., 0], to_chunks(beta[..., None])[..., 0])
    o = from_chunks(o)
    o = rms_norm(o, norm_w) * jax.nn.silu(z.astype(jnp.float32).reshape(bsz, l, GDN_V_HEADS, GDN_HEAD_DIM))
    return o.reshape(bsz, l, GDN_VAL).astype(h.dtype) @ w_out


def swiglu(h, w_gate_up, w_down):
    gate, up = jnp.split(h @ w_gate_up, 2, axis=-1)
    return (jax.nn.silu(gate) * up) @ w_down


def setup_inputs(seed: int = 0) -> dict:
    key = jax.random.key(seed)
    ks = jax.random.split(key, 20)
    f32 = jnp.float32
    n_gla = (DEPTH + 1) // 2
    n_gdn = DEPTH // 2

    def dense(k, shape, fan_in):
        return jax.random.normal(k, shape, f32) * fan_in ** -0.5

    def gain(k, shape):
        return 1.0 + 0.01 * jax.random.normal(k, shape, f32)

    x = jax.random.normal(ks[0], (BATCH, SEQ, D_MODEL), f32)
    gla_w_in = dense(ks[1], (n_gla, D_MODEL, GLA_IN), D_MODEL)
    gla_w_gate_up = dense(ks[2], (n_gla, GLA_RANK, GLA_KEY), GLA_RANK)
    gla_b_gate = 0.1 * jax.random.normal(ks[3], (n_gla, GLA_KEY), f32)
    gla_norm_w = gain(ks[4], (n_gla, GLA_DV))
    gla_w_out = dense(ks[5], (n_gla, GLA_VAL, D_MODEL), GLA_VAL)
    gdn_w_in = dense(ks[6], (n_gdn, D_MODEL, GDN_IN), D_MODEL)
    gdn_conv_w = dense(ks[7], (n_gdn, GDN_CONV, GDN_CONV_CH), GDN_CONV)
    gdn_a_log = jnp.log(jax.random.uniform(ks[8], (n_gdn, GDN_V_HEADS), f32, 1.0, 16.0))
    dt = jnp.exp(jax.random.uniform(ks[9], (n_gdn, GDN_V_HEADS), f32,
                                    math.log(1e-3), math.log(1e-1)))
    gdn_dt_bias = dt + jnp.log(-jnp.expm1(-dt))
    gdn_norm_w = gain(ks[10], (n_gdn, GDN_HEAD_DIM))
    gdn_w_out = dense(ks[11], (n_gdn, GDN_VAL, D_MODEL), GDN_VAL)
    mix_norm_w = gain(ks[12], (DEPTH, D_MODEL))
    ffn_norm_w = gain(ks[13], (DEPTH, D_MODEL))
    ffn_w_gate_up = dense(ks[14], (DEPTH, D_MODEL, 2 * D_FF), D_MODEL)
    ffn_w_down = dense(ks[15], (DEPTH, D_FF, D_MODEL), D_FF)
    final_norm_w = gain(ks[16], (D_MODEL,))
    return {
        'x': x,
        'gla_w_in': gla_w_in, 'gla_w_gate_up': gla_w_gate_up, 'gla_b_gate': gla_b_gate,
        'gla_norm_w': gla_norm_w, 'gla_w_out': gla_w_out,
        'gdn_w_in': gdn_w_in, 'gdn_conv_w': gdn_conv_w, 'gdn_a_log': gdn_a_log,
        'gdn_dt_bias': gdn_dt_bias, 'gdn_norm_w': gdn_norm_w, 'gdn_w_out': gdn_w_out,
        'mix_norm_w': mix_norm_w, 'ffn_norm_w': ffn_norm_w,
        'ffn_w_gate_up': ffn_w_gate_up, 'ffn_w_down': ffn_w_down,
        'final_norm_w': final_norm_w,
    }


def reference(x, gla_w_in, gla_w_gate_up, gla_b_gate, gla_norm_w, gla_w_out,
              gdn_w_in, gdn_conv_w, gdn_a_log, gdn_dt_bias, gdn_norm_w, gdn_w_out,
              mix_norm_w, ffn_norm_w, ffn_w_gate_up, ffn_w_down, final_norm_w):
    h = x
    for i in range(DEPTH):
        j = i // N_MIXERS
        hn = rms_norm(h, mix_norm_w[i])
        if i % N_MIXERS == 0:
            h = h + gla_mixer(hn, gla_w_in[j], gla_w_gate_up[j], gla_b_gate[j],
                              gla_norm_w[j], gla_w_out[j])
        else:
            h = h + gdn_mixer(hn, gdn_w_in[j], gdn_conv_w[j], gdn_a_log[j],
                              gdn_dt_bias[j], gdn_norm_w[j], gdn_w_out[j])
        h = h + swiglu(rms_norm(h, ffn_norm_w[i]), ffn_w_gate_up[i], ffn_w_down[i])
    return rms_norm(h, final_norm_w)
```

```python
import functools

import jax
import jax.numpy as jnp
from jax import lax
from jax.experimental import pallas as pl
from jax.experimental.pallas import tpu as pltpu

F32 = jnp.float32
BF16 = jnp.bfloat16
HIGHEST = lax.Precision.HIGHEST

NORM_EPS = 1e-6
L2_EPS = 1e-6
LANES = 128
CHUNK = 64
SUB = 16
GLA_TAU = 16.0
GDN_CONV = 4
CARRY_ROWS = 8
VMEM_LIMIT = 56 * 1024 * 1024


def _dot(a, b):
    return jnp.dot(a, b, preferred_element_type=F32)


def _dot_nt(a, b):
    return lax.dot_general(a, b, (((1,), (1,)), ((), ())), preferred_element_type=F32)


def _dot_tn(a, b):
    return lax.dot_general(a, b, (((0,), (0,)), ((), ())), preferred_element_type=F32)


def _dot_exact(a, b):
    return jnp.dot(a, b, precision=HIGHEST, preferred_element_type=F32)


def _rms(x, w):
    return x * lax.rsqrt(jnp.mean(x * x, axis=-1, keepdims=True) + NORM_EPS) * w


def _sigmoid(x):
    return 1.0 / (1.0 + jnp.exp(-x))


def _silu(x):
    return x * _sigmoid(x)


def _softplus(x):
    return jnp.maximum(x, 0.0) + jnp.log(1.0 + jnp.exp(-jnp.abs(x)))


def _tri_masks(n):
    row = lax.broadcasted_iota(jnp.int32, (n, n), 0)
    col = lax.broadcasted_iota(jnp.int32, (n, n), 1)
    return row >= col, row > col


def _params(sem):
    return pltpu.CompilerParams(dimension_semantics=sem, vmem_limit_bytes=VMEM_LIMIT)


def _gla_proj_kernel(x_ref, nw_ref, w_ref, wg_ref, wgu_ref, bg_ref, proj_ref, la_ref, xn_ref):
    @pl.when(pl.program_id(1) == 0)
    def _():
        xn = _rms(x_ref[...], nw_ref[...]).astype(BF16)
        xn_ref[...] = xn
        g_low = _dot(xn, wg_ref[...])
        z = _dot_exact(g_low, wgu_ref[...]) + bg_ref[...]
        la_ref[...] = -_softplus(-z) / GLA_TAU

    proj_ref[...] = _dot(xn_ref[...], w_ref[...]).astype(proj_ref.dtype)


def _gla_proj(x2, norm_w, w_main, w_g, w_gu, b_g, *, tm, tn):
    t, d = x2.shape
    n = w_main.shape[1]
    rank = w_g.shape[1]
    key = w_gu.shape[1]
    return pl.pallas_call(
        _gla_proj_kernel,
        grid=(t // tm, n // tn),
        in_specs=[
            pl.BlockSpec((tm, d), lambda i, j: (i, 0)),
            pl.BlockSpec((1, d), lambda i, j: (0, 0)),
            pl.BlockSpec((d, tn), lambda i, j: (0, j)),
            pl.BlockSpec((d, rank), lambda i, j: (0, 0)),
            pl.BlockSpec((rank, key), lambda i, j: (0, 0)),
            pl.BlockSpec((1, key), lambda i, j: (0, 0)),
        ],
        out_specs=[
            pl.BlockSpec((tm, tn), lambda i, j: (i, j)),
            pl.BlockSpec((tm, key), lambda i, j: (i, 0)),
        ],
        out_shape=[
            jax.ShapeDtypeStruct((t, n), BF16),
            jax.ShapeDtypeStruct((t, key), F32),
        ],
        scratch_shapes=[pltpu.VMEM((tm, d), BF16)],
        compiler_params=_params(("arbitrary", "arbitrary")),
        name="gla_proj",
    )(x2, norm_w, w_main, w_g, w_gu, b_g)


def _gla_scan_kernel(q_ref, k_ref, v_ref, r_ref, la_ref, nw_ref, o_ref, st_ref, *,
                     heads, dk, dv, n_chunks):
    @pl.when(pl.program_id(1) == 0)
    def _():
        st_ref[...] = jnp.zeros_like(st_ref)

    causal, _ = _tri_masks(CHUNK)
    tril = causal.astype(F32)
    krow = lax.broadcasted_iota(jnp.int32, (CHUNK, dk), 0)
    scale = dk ** -0.5
    nw = nw_ref[...]

    def chunk_body(c, carry):
        rows = pl.ds(pl.multiple_of(c * CHUNK, CHUNK), CHUNK)
        g_all = _dot_exact(tril, la_ref[rows, :])
        for h in range(heads):
            ks = slice(h * dk, (h + 1) * dk)
            vs = slice(h * dv, (h + 1) * dv)
            g = g_all[:, ks]
            q = q_ref[rows, ks].astype(F32) * scale
            k = k_ref[rows, ks].astype(F32)
            v = v_ref[rows, vs]
            blocks = []
            for b in range(CHUNK // SUB):
                ref_row = g[b * SUB:b * SUB + 1, :]
                qb = q[b * SUB:(b + 1) * SUB, :] * jnp.exp(g[b * SUB:(b + 1) * SUB, :] - ref_row)
                e = jnp.where(krow < (b + 1) * SUB, ref_row - g, 0.0)
                kb = k * jnp.exp(e)
                blocks.append(_dot_nt(qb.astype(BF16), kb.astype(BF16)))
            scores = jnp.where(causal, jnp.concatenate(blocks, axis=0), 0.0)
            o = _dot(scores.astype(BF16), v)
            st = st_ref[h]
            o = o + _dot_nt((q * jnp.exp(g)).astype(BF16), st.astype(BF16))
            g_last = g[CHUNK - 1:CHUNK, :]
            k_dec = (k * jnp.exp(g_last - g)).astype(BF16)
            st_ref[h] = st * jnp.exp(g_last) + _dot_tn(v, k_dec)
            y = _rms(o, nw) * _silu(r_ref[rows, vs].astype(F32))
            o_ref[rows, vs] = y.astype(o_ref.dtype)
        return carry

    lax.fori_loop(0, n_chunks, chunk_body, 0)


def _gla_scan(proj, log_a, norm_w, *, batch, seq, heads, dk, dv, lb):
    key = heads * dk
    val = heads * dv
    nb = seq // lb
    kern = functools.partial(_gla_scan_kernel, heads=heads, dk=dk, dv=dv, n_chunks=lb // CHUNK)
    row = lambda b, l: b * nb + l
    return pl.pallas_call(
        kern,
        grid=(batch, nb),
        in_specs=[
            pl.BlockSpec((lb, key), lambda b, l: (row(b, l), 0)),
            pl.BlockSpec((lb, key), lambda b, l: (row(b, l), 1)),
            pl.BlockSpec((lb, val), lambda b, l: (row(b, l), 2 * key // val)),
            pl.BlockSpec((lb, val), lambda b, l: (row(b, l), 2 * key // val + 1)),
            pl.BlockSpec((lb, key), lambda b, l: (row(b, l), 0)),
            pl.BlockSpec((1, dv), lambda b, l: (0, 0)),
        ],
        out_specs=pl.BlockSpec((lb, val), lambda b, l: (row(b, l), 0)),
        out_shape=jax.ShapeDtypeStruct((batch * seq, val), BF16),
        scratch_shapes=[pltpu.VMEM((heads, dv, dk), F32)],
        compiler_params=_params(("arbitrary", "arbitrary")),
        name="gla_scan",
    )(proj, proj, proj, proj, log_a, norm_w)


def _post_kernel(h_ref, o_ref, wout_ref, nw_ref, wgu_ref, wd_ref, fw_ref, out_ref, *, d_ff, final):
    h1 = h_ref[...] + _dot(o_ref[...], wout_ref[...])
    xn = _rms(h1, nw_ref[...]).astype(BF16)
    gu = _dot(xn, wgu_ref[...])
    act = (_silu(gu[:, :d_ff]) * gu[:, d_ff:]).astype(BF16)
    out = h1 + _dot(act, wd_ref[...])
    if final:
        out = _rms(out, fw_ref[...])
    out_ref[...] = out


def _post(h2, o2, w_out, norm_w, w_gu, w_d, final_w, *, tm, final):
    t, d = h2.shape
    val = o2.shape[1]
    d_ff = w_d.shape[0]
    const = lambda shape: pl.BlockSpec(shape, lambda i: (0, 0), pipeline_mode=pl.Buffered(1))
    return pl.pallas_call(
        functools.partial(_post_kernel, d_ff=d_ff, final=final),
        grid=(t // tm,),
        in_specs=[
            pl.BlockSpec((tm, d), lambda i: (i, 0)),
            pl.BlockSpec((tm, val), lambda i: (i, 0)),
            const((val, d)),
            const((1, d)),
            const((d, 2 * d_ff)),
            const((d_ff, d)),
            const((1, d)),
        ],
        out_specs=pl.BlockSpec((tm, d), lambda i: (i, 0)),
        out_shape=jax.ShapeDtypeStruct((t, d), F32),
        compiler_params=_params(("arbitrary",)),
        name="post_final" if final else "post",
    )(h2, o2, w_out, norm_w, w_gu, w_d, final_w)


def _gdn_proj_kernel(x_ref, nw_ref, w_ref, wba_ref, alog_ref, dt_ref, cw_ref,
                     proj_ref, gb_ref, xn_ref, ybuf_ref, carry_ref, *,
                     tiles_per_seq, n_qk_tiles, n_conv_tiles, heads, head_dim, q_tile_is_first):
    i = pl.program_id(0)
    j = pl.program_id(1)
    tm, tn = proj_ref.shape

    @pl.when(j == 0)
    def _():
        xn = _rms(x_ref[...], nw_ref[...]).astype(BF16)
        xn_ref[...] = xn
        ba = _dot(xn, wba_ref[...])
        lane = lax.broadcasted_iota(jnp.int32, ba.shape, 1)
        beta = _sigmoid(ba)
        g = -jnp.exp(alog_ref[...]) * _softplus(ba + dt_ref[...])
        gb_ref[...] = jnp.where(lane < heads, beta, jnp.where(lane < 2 * heads, g, 0.0))

    y = _dot(xn_ref[...], w_ref[...])

    def conv_silu():
        jc = jnp.minimum(j, n_conv_tiles - 1)

        @pl.when(i % tiles_per_seq == 0)
        def _():
            carry_ref[jc] = jnp.zeros((CARRY_ROWS, tn), F32)

        ybuf_ref[0:CARRY_ROWS, :] = carry_ref[jc]
        ybuf_ref[CARRY_ROWS:, :] = y
        carry_ref[jc] = y[tm - CARRY_ROWS:, :]
        cw = cw_ref[...]
        acc = y * cw[GDN_CONV - 1:GDN_CONV, :]
        for s in range(1, GDN_CONV):
            acc = acc + ybuf_ref[CARRY_ROWS - s:CARRY_ROWS - s + tm, :] * cw[GDN_CONV - 1 - s:GDN_CONV - s, :]
        return _silu(acc)

    @pl.when(j < n_qk_tiles)
    def _():
        c = conv_silu()
        post = jnp.where(j < q_tile_is_first, head_dim ** -0.5, 1.0)
        for hh in range(tn // head_dim):
            seg = c[:, hh * head_dim:(hh + 1) * head_dim]
            inv = lax.rsqrt(jnp.sum(seg * seg, axis=-1, keepdims=True) + L2_EPS)
            proj_ref[:, hh * head_dim:(hh + 1) * head_dim] = (seg * (inv * post)).astype(proj_ref.dtype)

    @pl.when(jnp.logical_and(j >= n_qk_tiles, j < n_conv_tiles))
    def _():
        proj_ref[...] = conv_silu().astype(proj_ref.dtype)

    @pl.when(j >= n_conv_tiles)
    def _():
        proj_ref[...] = y.astype(proj_ref.dtype)


def _gdn_proj(x2, norm_w, w_main, w_ba, a_log, dt_bias, conv_w, *, seq, tm, tn, key, conv_ch,
              heads, head_dim):
    t, d = x2.shape
    n = w_main.shape[1]
    n_conv_tiles = conv_ch // tn
    kern = functools.partial(
        _gdn_proj_kernel, tiles_per_seq=seq // tm, n_qk_tiles=2 * key // tn,
        n_conv_tiles=n_conv_tiles, heads=heads, head_dim=head_dim, q_tile_is_first=key // tn)
    return pl.pallas_call(
        kern,
        grid=(t // tm, n // tn),
        in_specs=[
            pl.BlockSpec((tm, d), lambda i, j: (i, 0)),
            pl.BlockSpec((1, d), lambda i, j: (0, 0)),
            pl.BlockSpec((d, tn), lambda i, j: (0, j)),
            pl.BlockSpec((d, LANES), lambda i, j: (0, 0)),
            pl.BlockSpec((1, LANES), lambda i, j: (0, 0)),
            pl.BlockSpec((1, LANES), lambda i, j: (0, 0)),
            pl.BlockSpec((GDN_CONV, tn), lambda i, j: (0, jnp.minimum(j, n_conv_tiles - 1))),
        ],
        out_specs=[
            pl.BlockSpec((tm, tn), lambda i, j: (i, j)),
            pl.BlockSpec((tm, LANES), lambda i, j: (i, 0)),
        ],
        out_shape=[
            jax.ShapeDtypeStruct((t, n), BF16),
            jax.ShapeDtypeStruct((t, LANES), F32),
        ],
        scratch_shapes=[
            pltpu.VMEM((tm, d), BF16),
            pltpu.VMEM((tm + CARRY_ROWS, tn), F32),
            pltpu.VMEM((n_conv_tiles, CARRY_ROWS, tn), F32),
        ],
        compiler_params=_params(("arbitrary", "arbitrary")),
        name="gdn_proj",
    )(x2, norm_w, w_main, w_ba, a_log, dt_bias, conv_w)


def _unit_lower_inverse(a_strict):
    n = a_strict.shape[0]
    row = lax.broadcasted_iota(jnp.int32, (n, n), 0)
    col = lax.broadcasted_iota(jnp.int32, (n, n), 1)
    p = -a_strict
    t = jnp.where(row == col, 1.0, 0.0) + p
    p = _dot_exact(p, p)
    span = 2
    while 2 * span < n:
        both = _dot_exact(p, jnp.concatenate([t, p], axis=1))
        t = t + both[:, :n]
        p = both[:, n:]
        span *= 2
    return t + _dot_exact(p, t)


def _gdn_scan_kernel(q_ref, k_ref, v_ref, z_ref, gb_ref, nw_ref, o_ref, s_ref, *,
                     qk_heads, rep, hd, n_chunks):
    v_heads = qk_heads * rep

    @pl.when(pl.program_id(1) == 0)
    def _():
        s_ref[...] = jnp.zeros_like(s_ref)

    causal, strict = _tri_masks(CHUNK)
    tril = causal.astype(F32)
    nw = nw_ref[...]

    def chunk_body(c, carry):
        rows = pl.ds(pl.multiple_of(c * CHUNK, CHUNK), CHUNK)
        gb = gb_ref[rows, :]
        g_cum = _dot_exact(tril, gb)
        g_cum_t = g_cum.T
        for hq in range(qk_heads):
            qs = slice(hq * hd, (hq + 1) * hd)
            q = q_ref[rows, qs]
            k = k_ref[rows, qs]
            kk = _dot_nt(k, k)
            qk = _dot_nt(q, k)
            qf = q.astype(F32)
            kf = k.astype(F32)
            for r in range(rep):
                hv = hq * rep + r
                vs = slice(hv * hd, (hv + 1) * hd)
                g_col = g_cum[:, v_heads + hv:v_heads + hv + 1]
                g_row = g_cum_t[v_heads + hv:v_heads + hv + 1, :]
                beta = gb[:, hv:hv + 1]
                decay = jnp.where(causal, jnp.exp(jnp.where(causal, g_col - g_row, 0.0)), 0.0)
                a = jnp.where(strict, kk * beta * decay, 0.0)
                t_inv = _unit_lower_inverse(a).astype(BF16)
                e_g = jnp.exp(g_col)
                vf = v_ref[rows, vs].astype(F32)
                rhs = jnp.concatenate([vf * beta, kf * (beta * e_g)], axis=1).astype(BF16)
                uw = _dot(t_inv, rhs)
                s = s_ref[hv]
                ws_qs = _dot(jnp.concatenate([uw[:, hd:], qf * e_g], axis=0).astype(BF16),
                             s.astype(BF16))
                v_new = (uw[:, :hd] - ws_qs[:CHUNK]).astype(BF16)
                attn = jnp.where(causal, qk * decay, 0.0).astype(BF16)
                o = ws_qs[CHUNK:] + _dot(attn, v_new)
                g_last = g_col[CHUNK - 1:CHUNK, :]
                k_dec = (kf * jnp.exp(g_last - g_col)).astype(BF16)
                s_ref[hv] = s * jnp.exp(g_last) + _dot_tn(k_dec, v_new)
                y = _rms(o, nw) * _silu(z_ref[rows, vs].astype(F32))
                o_ref[rows, vs] = y.astype(o_ref.dtype)
        return carry

    lax.fori_loop(0, n_chunks, chunk_body, 0)


def _gdn_scan(proj, gb, norm_w, *, batch, seq, qk_heads, rep, hd, lb):
    key = qk_heads * hd
    val = key * rep
    nb = seq // lb
    kern = functools.partial(_gdn_scan_kernel, qk_heads=qk_heads, rep=rep, hd=hd,
                             n_chunks=lb // CHUNK)
    row = lambda b, l: b * nb + l
    return pl.pallas_call(
        kern,
        grid=(batch, nb),
        in_specs=[
            pl.BlockSpec((lb, key), lambda b, l: (row(b, l), 0)),
            pl.BlockSpec((lb, key), lambda b, l: (row(b, l), 1)),
            pl.BlockSpec((lb, val), lambda b, l: (row(b, l), 2 * key // val)),
            pl.BlockSpec((lb, val), lambda b, l: (row(b, l), 2 * key // val + 1)),
            pl.BlockSpec((lb, LANES), lambda b, l: (row(b, l), 0)),
            pl.BlockSpec((1, hd), lambda b, l: (0, 0)),
        ],
        out_specs=pl.BlockSpec((lb, val), lambda b, l: (row(b, l), 0)),
        out_shape=jax.ShapeDtypeStruct((batch * seq, val), BF16),
        scratch_shapes=[pltpu.VMEM((qk_heads * rep, hd, hd), F32)],
        compiler_params=_params(("arbitrary", "arbitrary")),
        name="gdn_scan",
    )(proj, proj, proj, proj, gb, norm_w)


def _pad_lanes(a):
    return jnp.pad(a, ((0, 0), (0, LANES - a.shape[1])))


def kernel(x, gla_w_in, gla_w_gate_up, gla_b_gate, gla_norm_w, gla_w_out, gdn_w_in, gdn_conv_w,
           gdn_a_log, gdn_dt_bias, gdn_norm_w, gdn_w_out, mix_norm_w, ffn_norm_w, ffn_w_gate_up,
           ffn_w_down, final_norm_w):
    batch, seq, d = x.shape
    depth = mix_norm_w.shape[0]
    t = batch * seq

    gla_rank, gla_key = gla_w_gate_up.shape[1:]
    gla_dv = gla_norm_w.shape[1]
    gla_val = gla_w_out.shape[1]
    gla_heads = gla_val // gla_dv
    gla_dk = gla_key // gla_heads
    gla_main = 2 * gla_key + 2 * gla_val

    hd = gdn_norm_w.shape[1]
    gdn_v_heads = gdn_a_log.shape[1]
    gdn_val = gdn_w_out.shape[1]
    gdn_conv_ch = gdn_conv_w.shape[2]
    gdn_key = (gdn_conv_ch - gdn_val) // 2
    gdn_qk_heads = gdn_key // hd
    gdn_main = gdn_conv_ch + gdn_val

    tm = min(1024, seq)
    tn = 1024
    lb = min(256, seq)
    row = lambda a: a.reshape(1, -1)

    h = x.reshape(t, d)
    for i in range(depth):
        j = i // 2
        if i % 2 == 0:
            w_in = gla_w_in[j]
            proj, log_a = _gla_proj(
                h, row(mix_norm_w[i]), w_in[:, :gla_main].astype(BF16),
                w_in[:, gla_main:].astype(BF16), gla_w_gate_up[j], row(gla_b_gate[j]),
                tm=tm, tn=tn)
            o = _gla_scan(proj, log_a, row(gla_norm_w[j]), batch=batch, seq=seq,
                          heads=gla_heads, dk=gla_dk, dv=gla_dv, lb=lb)
            w_out = gla_w_out[j]
        else:
            w_in = gdn_w_in[j]
            zeros = jnp.zeros((gdn_v_heads,), F32)
            proj, gb = _gdn_proj(
                h, row(mix_norm_w[i]), w_in[:, :gdn_main].astype(BF16),
                _pad_lanes(w_in[:, gdn_main:]).astype(BF16),
                _pad_lanes(row(jnp.concatenate([zeros, gdn_a_log[j]]))),
                _pad_lanes(row(jnp.concatenate([zeros, gdn_dt_bias[j]]))),
                gdn_conv_w[j], seq=seq, tm=tm, tn=tn, key=gdn_key, conv_ch=gdn_conv_ch,
                heads=gdn_v_heads, head_dim=hd)
            o = _gdn_scan(proj, gb, row(gdn_norm_w[j]), batch=batch, seq=seq,
                          qk_heads=gdn_qk_heads, rep=gdn_v_heads // gdn_qk_heads, hd=hd, lb=lb)
            w_out = gdn_w_out[j]
        h = _post(h, o, w_out.astype(BF16), row(ffn_norm_w[i]), ffn_w_gate_up[i].astype(BF16),
                  ffn_w_down[i].astype(BF16), row(final_norm_w), tm=min(512, seq),
                  final=(i == depth - 1))
    return h.reshape(batch, seq, d)
```

```python
import functools

import jax
import jax.numpy as jnp
from jax import lax
from jax.experimental import pallas as pl
from jax.experimental.pallas import tpu as pltpu

F32 = jnp.float32
BF16 = jnp.bfloat16
HIGHEST = lax.Precision.HIGHEST

NORM_EPS = 1e-6
L2_EPS = 1e-6
LANES = 128
CHUNK = 64
SUB = 16
GLA_TAU = 16.0
GDN_CONV = 4
CARRY_ROWS = 8
VMEM_LIMIT = 56 * 1024 * 1024


def _dot(a, b):
    return jnp.dot(a, b, preferred_element_type=F32)


def _dot_nt(a, b):
    return lax.dot_general(a, b, (((1,), (1,)), ((), ())), preferred_element_type=F32)


def _dot_tn(a, b):
    return lax.dot_general(a, b, (((0,), (0,)), ((), ())), preferred_element_type=F32)


def _dot_exact(a, b):
    return jnp.dot(a, b, precision=HIGHEST, preferred_element_type=F32)


def _rms(x, w):
    return x * lax.rsqrt(jnp.mean(x * x, axis=-1, keepdims=True) + NORM_EPS) * w


def _sigmoid(x):
    return 1.0 / (1.0 + jnp.exp(-x))


def _silu(x):
    return x * _sigmoid(x)


def _softplus(x):
    return jnp.maximum(x, 0.0) + jnp.log(1.0 + jnp.exp(-jnp.abs(x)))


def _tri_masks(n):
    row = lax.broadcasted_iota(jnp.int32, (n, n), 0)
    col = lax.broadcasted_iota(jnp.int32, (n, n), 1)
    return row >= col, row > col


def _params(sem):
    return pltpu.CompilerParams(dimension_semantics=sem, vmem_limit_bytes=VMEM_LIMIT)


def _gla_proj_kernel(x_ref, nw_ref, w_ref, wg_ref, wgu_ref, bg_ref, proj_ref, la_ref, xn_ref):
    @pl.when(pl.program_id(1) == 0)
    def _():
        xn = _rms(x_ref[...], nw_ref[...]).astype(BF16)
        xn_ref[...] = xn
        g_low = _dot(xn, wg_ref[...])
        z = _dot_exact(g_low, wgu_ref[...]) + bg_ref[...]
        la_ref[...] = -_softplus(-z) / GLA_TAU

    proj_ref[...] = _dot(xn_ref[...], w_ref[...]).astype(proj_ref.dtype)


def _gla_proj(x2, norm_w, w_main, w_g, w_gu, b_g, *, tm, tn):
    t, d = x2.shape
    n = w_main.shape[1]
    rank = w_g.shape[1]
    key = w_gu.shape[1]
    return pl.pallas_call(
        _gla_proj_kernel,
        grid=(t // tm, n // tn),
        in_specs=[
            pl.BlockSpec((tm, d), lambda i, j: (i, 0)),
            pl.BlockSpec((1, d), lambda i, j: (0, 0)),
            pl.BlockSpec((d, tn), lambda i, j: (0, j)),
            pl.BlockSpec((d, rank), lambda i, j: (0, 0)),
            pl.BlockSpec((rank, key), lambda i, j: (0, 0)),
            pl.BlockSpec((1, key), lambda i, j: (0, 0)),
        ],
        out_specs=[
            pl.BlockSpec((tm, tn), lambda i, j: (i, j)),
            pl.BlockSpec((tm, key), lambda i, j: (i, 0)),
        ],
        out_shape=[
            jax.ShapeDtypeStruct((t, n), BF16),
            jax.ShapeDtypeStruct((t, key), F32),
        ],
        scratch_shapes=[pltpu.VMEM((tm, d), BF16)],
        compiler_params=_params(("arbitrary", "arbitrary")),
        name="gla_proj",
    )(x2, norm_w, w_main, w_g, w_gu, b_g)


def _gla_scan_kernel(q_ref, k_ref, v_ref, r_ref, la_ref, nw_ref, o_ref, st_ref, *,
                     heads, dk, dv, n_chunks):
    @pl.when(pl.program_id(1) == 0)
    def _():
        st_ref[...] = jnp.zeros_like(st_ref)

    causal, _ = _tri_masks(CHUNK)
    tril = causal.astype(F32)
    krow = lax.broadcasted_iota(jnp.int32, (CHUNK, dk), 0)
    scale = dk ** -0.5
    nw = nw_ref[...]

    def chunk_body(c, carry):
        rows = pl.ds(pl.multiple_of(c * CHUNK, CHUNK), CHUNK)
        g_all = _dot_exact(tril, la_ref[rows, :])
        hs = range(heads)
        g = [g_all[:, h * dk:(h + 1) * dk] for h in hs]
        q = [q_ref[rows, h * dk:(h + 1) * dk].astype(F32) * scale for h in hs]
        k = [k_ref[rows, h * dk:(h + 1) * dk].astype(F32) for h in hs]
        v = [v_ref[rows, h * dv:(h + 1) * dv] for h in hs]
        blocks = [[] for _ in hs]
        for b in range(CHUNK // SUB):
            for h in hs:
                ref_row = g[h][b * SUB:b * SUB + 1, :]
                qb = q[h][b * SUB:(b + 1) * SUB, :] * jnp.exp(g[h][b * SUB:(b + 1) * SUB, :] - ref_row)
                e = jnp.where(krow < (b + 1) * SUB, ref_row - g[h], 0.0)
                kb = k[h] * jnp.exp(e)
                blocks[h].append(_dot_nt(qb.astype(BF16), kb.astype(BF16)))
        scores = [jnp.where(causal, jnp.concatenate(blocks[h], axis=0), 0.0).astype(BF16) for h in hs]
        st = [st_ref[h] for h in hs]
        o = [_dot(scores[h], v[h]) + _dot_nt((q[h] * jnp.exp(g[h])).astype(BF16), st[h].astype(BF16))
             for h in hs]
        g_last = [g[h][CHUNK - 1:CHUNK, :] for h in hs]
        k_dec = [(k[h] * jnp.exp(g_last[h] - g[h])).astype(BF16) for h in hs]
        for h in hs:
            st_ref[h] = st[h] * jnp.exp(g_last[h]) + _dot_tn(v[h], k_dec[h])
            y = _rms(o[h], nw) * _silu(r_ref[rows, h * dv:(h + 1) * dv].astype(F32))
            o_ref[rows, h * dv:(h + 1) * dv] = y.astype(o_ref.dtype)
        return carry

    lax.fori_loop(0, n_chunks, chunk_body, 0)


def _gla_scan(proj, log_a, norm_w, *, batch, seq, heads, dk, dv, lb):
    key = heads * dk
    val = heads * dv
    nb = seq // lb
    kern = functools.partial(_gla_scan_kernel, heads=heads, dk=dk, dv=dv, n_chunks=lb // CHUNK)
    row = lambda b, l: b * nb + l
    return pl.pallas_call(
        kern,
        grid=(batch, nb),
        in_specs=[
            pl.BlockSpec((lb, key), lambda b, l: (row(b, l), 0)),
            pl.BlockSpec((lb, key), lambda b, l: (row(b, l), 1)),
            pl.BlockSpec((lb, val), lambda b, l: (row(b, l), 2 * key // val)),
            pl.BlockSpec((lb, val), lambda b, l: (row(b, l), 2 * key // val + 1)),
            pl.BlockSpec((lb, key), lambda b, l: (row(b, l), 0)),
            pl.BlockSpec((1, dv), lambda b, l: (0, 0)),
        ],
        out_specs=pl.BlockSpec((lb, val), lambda b, l: (row(b, l), 0)),
        out_shape=jax.ShapeDtypeStruct((batch * seq, val), BF16),
        scratch_shapes=[pltpu.VMEM((heads, dv, dk), F32)],
        compiler_params=_params(("arbitrary", "arbitrary")),
        name="gla_scan",
    )(proj, proj, proj, proj, log_a, norm_w)


def _post_kernel(h_ref, o_ref, wout_ref, nw_ref, wgu_ref, wd_ref, fw_ref, out_ref, *, d_ff, final):
    h1 = h_ref[...] + _dot(o_ref[...], wout_ref[...])
    xn = _rms(h1, nw_ref[...]).astype(BF16)
    gu = _dot(xn, wgu_ref[...])
    act = (_silu(gu[:, :d_ff]) * gu[:, d_ff:]).astype(BF16)
    out = h1 + _dot(act, wd_ref[...])
    if final:
        out = _rms(out, fw_ref[...])
    out_ref[...] = out


def _post(h2, o2, w_out, norm_w, w_gu, w_d, final_w, *, tm, final):
    t, d = h2.shape
    val = o2.shape[1]
    d_ff = w_d.shape[0]
    const = lambda shape: pl.BlockSpec(shape, lambda i: (0, 0), pipeline_mode=pl.Buffered(1))
    return pl.pallas_call(
        functools.partial(_post_kernel, d_ff=d_ff, final=final),
        grid=(t // tm,),
        in_specs=[
            pl.BlockSpec((tm, d), lambda i: (i, 0)),
            pl.BlockSpec((tm, val), lambda i: (i, 0)),
            const((val, d)),
            const((1, d)),
            const((d, 2 * d_ff)),
            const((d_ff, d)),
            const((1, d)),
        ],
        out_specs=pl.BlockSpec((tm, d), lambda i: (i, 0)),
        out_shape=jax.ShapeDtypeStruct((t, d), F32),
        compiler_params=_params(("arbitrary",)),
        name="post_final" if final else "post",
    )(h2, o2, w_out, norm_w, w_gu, w_d, final_w)


def _gdn_proj_kernel(x_ref, nw_ref, w_ref, wba_ref, alog_ref, dt_ref, cw_ref,
                     proj_ref, gb_ref, xn_ref, ybuf_ref, carry_ref, *,
                     tiles_per_seq, n_qk_tiles, n_conv_tiles, heads, head_dim, q_tile_is_first):
    i = pl.program_id(0)
    j = pl.program_id(1)
    tm, tn = proj_ref.shape

    @pl.when(j == 0)
    def _():
        xn = _rms(x_ref[...], nw_ref[...]).astype(BF16)
        xn_ref[...] = xn
        ba = _dot(xn, wba_ref[...])
        lane = lax.broadcasted_iota(jnp.int32, ba.shape, 1)
        beta = _sigmoid(ba)
        g = -jnp.exp(alog_ref[...]) * _softplus(ba + dt_ref[...])
        gb_ref[...] = jnp.where(lane < heads, g, jnp.where(lane < 2 * heads, beta, 0.0))

    y = _dot(xn_ref[...], w_ref[...])

    def conv_silu():
        jc = jnp.minimum(j, n_conv_tiles - 1)

        @pl.when(i % tiles_per_seq == 0)
        def _():
            carry_ref[jc] = jnp.zeros((CARRY_ROWS, tn), F32)

        ybuf_ref[0:CARRY_ROWS, :] = carry_ref[jc]
        ybuf_ref[CARRY_ROWS:, :] = y
        carry_ref[jc] = y[tm - CARRY_ROWS:, :]
        cw = cw_ref[...]
        acc = y * cw[GDN_CONV - 1:GDN_CONV, :]
        for s in range(1, GDN_CONV):
            acc = acc + ybuf_ref[CARRY_ROWS - s:CARRY_ROWS - s + tm, :] * cw[GDN_CONV - 1 - s:GDN_CONV - s, :]
        return _silu(acc)

    @pl.when(j < n_qk_tiles)
    def _():
        c = conv_silu()
        post = jnp.where(j < q_tile_is_first, head_dim ** -0.5, 1.0)
        for hh in range(tn // head_dim):
            seg = c[:, hh * head_dim:(hh + 1) * head_dim]
            inv = lax.rsqrt(jnp.sum(seg * seg, axis=-1, keepdims=True) + L2_EPS)
            proj_ref[:, hh * head_dim:(hh + 1) * head_dim] = (seg * (inv * post)).astype(proj_ref.dtype)

    @pl.when(jnp.logical_and(j >= n_qk_tiles, j < n_conv_tiles))
    def _():
        proj_ref[...] = conv_silu().astype(proj_ref.dtype)

    @pl.when(j >= n_conv_tiles)
    def _():
        proj_ref[...] = y.astype(proj_ref.dtype)


def _gdn_proj(x2, norm_w, w_main, w_ba, a_log, dt_bias, conv_w, *, seq, tm, tn, key, conv_ch,
              heads, head_dim):
    t, d = x2.shape
    n = w_main.shape[1]
    n_conv_tiles = conv_ch // tn
    kern = functools.partial(
        _gdn_proj_kernel, tiles_per_seq=seq // tm, n_qk_tiles=2 * key // tn,
        n_conv_tiles=n_conv_tiles, heads=heads, head_dim=head_dim, q_tile_is_first=key // tn)
    return pl.pallas_call(
        kern,
        grid=(t // tm, n // tn),
        in_specs=[
            pl.BlockSpec((tm, d), lambda i, j: (i, 0)),
            pl.BlockSpec((1, d), lambda i, j: (0, 0)),
            pl.BlockSpec((d, tn), lambda i, j: (0, j)),
            pl.BlockSpec((d, LANES), lambda i, j: (0, 0)),
            pl.BlockSpec((1, LANES), lambda i, j: (0, 0)),
            pl.BlockSpec((1, LANES), lambda i, j: (0, 0)),
            pl.BlockSpec((GDN_CONV, tn), lambda i, j: (0, jnp.minimum(j, n_conv_tiles - 1))),
        ],
        out_specs=[
            pl.BlockSpec((tm, tn), lambda i, j: (i, j)),
            pl.BlockSpec((tm, LANES), lambda i, j: (i, 0)),
        ],
        out_shape=[
            jax.ShapeDtypeStruct((t, n), BF16),
            jax.ShapeDtypeStruct((t, LANES), F32),
        ],
        scratch_shapes=[
            pltpu.VMEM((tm, d), BF16),
            pltpu.VMEM((tm + CARRY_ROWS, tn), F32),
            pltpu.VMEM((n_conv_tiles, CARRY_ROWS, tn), F32),
        ],
        compiler_params=_params(("arbitrary", "arbitrary")),
        name="gdn_proj",
    )(x2, norm_w, w_main, w_ba, a_log, dt_bias, conv_w)


def _gdn_scan_kernel(q_ref, k_ref, v_ref, z_ref, gb_ref, nw_ref, o_ref, s_ref, *,
                     qk_heads, rep, hd, n_chunks, group):
    v_heads = qk_heads * rep
    R = rep * CHUNK

    @pl.when(pl.program_id(1) == 0)
    def _():
        s_ref[...] = jnp.zeros_like(s_ref)

    causal, _ = _tri_masks(CHUNK)
    tril = causal.astype(F32)
    row = lax.broadcasted_iota(jnp.int32, (R, R), 0)
    col = lax.broadcasted_iota(jnp.int32, (R, R), 1)
    same = (row // CHUNK) == (col // CHUNK)
    causal2 = jnp.logical_and(same, row >= col)
    strict2 = jnp.logical_and(same, row > col)
    eye2 = jnp.where(row == col, 1.0, 0.0)
    rblk = lax.broadcasted_iota(jnp.int32, (R, hd), 0) // CHUNK
    nw = nw_ref[...]

    def stack_heads(x):
        return jnp.concatenate(
            [x if r == 0 else pltpu.roll(x, LANES - r * qk_heads, 1) for r in range(rep)], axis=0)

    def pair_rows(ref, rows, p):
        return jnp.concatenate(
            [ref[rows, (p * rep + r) * hd:(p * rep + r + 1) * hd] for r in range(rep)], axis=0)

    def diag(x, base):
        return jnp.concatenate(
            [x[base + r * CHUNK:base + (r + 1) * CHUNK, r * hd:(r + 1) * hd] for r in range(rep)],
            axis=0)

    def chunk_body(c, carry):
        rows = pl.ds(pl.multiple_of(c * CHUNK, CHUNK), CHUNK)
        gb = gb_ref[rows, :]
        g_cum = _dot_exact(tril, gb)
        g_st = stack_heads(g_cum)
        g_st_t = g_st.T
        b_st = stack_heads(gb)
        for p0 in range(0, qk_heads, group):
            ps = list(range(p0, min(p0 + group, qk_heads)))
            n = range(len(ps))
            q2 = [jnp.concatenate([q_ref[rows, p * hd:(p + 1) * hd]] * rep, axis=0) for p in ps]
            k2 = [jnp.concatenate([k_ref[rows, p * hd:(p + 1) * hd]] * rep, axis=0) for p in ps]
            kk = [_dot_nt(k2[i], k2[i]) for i in n]
            qk = [_dot_nt(q2[i], k2[i]) for i in n]
            g_col = [g_st[:, p:p + 1] for p in ps]
            beta = [b_st[:, v_heads + p:v_heads + p + 1] for p in ps]
            decay = [jnp.where(causal2, jnp.exp(jnp.where(causal2, g_col[i] - g_st_t[p:p + 1, :], 0.0)), 0.0)
                     for i, p in enumerate(ps)]
            pm = [jnp.where(strict2, -(kk[i] * beta[i] * decay[i]), 0.0) for i in n]
            t = [eye2 + pm[i] for i in n]
            pmb = [x.astype(BF16) for x in pm]
            pm = [_dot(pmb[i], pmb[i]) for i in n]
            span = 2
            while 2 * span < CHUNK:
                both = [_dot(pm[i].astype(BF16), jnp.concatenate([t[i], pm[i]], axis=1).astype(BF16))
                        for i in n]
                t = [t[i] + both[i][:, :R] for i in n]
                pm = [both[i][:, R:] for i in n]
                span *= 2
            t = [(t[i] + _dot(pm[i].astype(BF16), t[i].astype(BF16))).astype(BF16) for i in n]
            e_g = [jnp.exp(g_col[i]) for i in n]
            kf = [k2[i].astype(F32) for i in n]
            rhs = [jnp.concatenate([pair_rows(v_ref, rows, p).astype(F32) * beta[i],
                                    kf[i] * (beta[i] * e_g[i])], axis=1).astype(BF16)
                   for i, p in enumerate(ps)]
            uw = [_dot(t[i], rhs[i]) for i in n]
            s = [s_ref[p] for p in ps]
            res = [_dot(jnp.concatenate([uw[i][:, hd:], q2[i].astype(F32) * e_g[i]], axis=0).astype(BF16),
                        s[i].astype(BF16)) for i in n]
            v_new = [uw[i][:, :hd] - diag(res[i], 0) for i in n]
            attn = [jnp.where(causal2, qk[i] * decay[i], 0.0).astype(BF16) for i in n]
            o = [diag(res[i], R) + _dot(attn[i], v_new[i].astype(BF16)) for i in n]
            g_last = [[g_col[i][(r + 1) * CHUNK - 1:(r + 1) * CHUNK, :] for r in range(rep)] for i in n]
            g_last_rows = [jnp.concatenate([jnp.broadcast_to(g, (CHUNK, 1)) for g in g_last[i]], axis=0)
                           for i in n]
            k_dec = [(kf[i] * jnp.exp(g_last_rows[i] - g_col[i])).astype(BF16) for i in n]
            v_blk = [jnp.concatenate([jnp.where(rblk == r, v_new[i], 0.0) for r in range(rep)],
                                     axis=1).astype(BF16) for i in n]
            e_last = [jnp.concatenate([jnp.broadcast_to(jnp.exp(g), (1, hd)) for g in g_last[i]], axis=1)
                      for i in n]
            for i, p in enumerate(ps):
                s_ref[p] = s[i] * e_last[i] + _dot_tn(k_dec[i], v_blk[i])
                y = _rms(o[i], nw) * _silu(pair_rows(z_ref, rows, p).astype(F32))
                for r in range(rep):
                    hv = p * rep + r
                    o_ref[rows, hv * hd:(hv + 1) * hd] = y[r * CHUNK:(r + 1) * CHUNK].astype(o_ref.dtype)
        return carry

    lax.fori_loop(0, n_chunks, chunk_body, 0)


def _gdn_scan(proj, gb, norm_w, *, batch, seq, qk_heads, rep, hd, lb, group):
    key = qk_heads * hd
    val = key * rep
    nb = seq // lb
    kern = functools.partial(_gdn_scan_kernel, qk_heads=qk_heads, rep=rep, hd=hd,
                             n_chunks=lb // CHUNK, group=group)
    row = lambda b, l: b * nb + l
    return pl.pallas_call(
        kern,
        grid=(batch, nb),
        in_specs=[
            pl.BlockSpec((lb, key), lambda b, l: (row(b, l), 0)),
            pl.BlockSpec((lb, key), lambda b, l: (row(b, l), 1)),
            pl.BlockSpec((lb, val), lambda b, l: (row(b, l), 2 * key // val)),
            pl.BlockSpec((lb, val), lambda b, l: (row(b, l), 2 * key // val + 1)),
            pl.BlockSpec((lb, LANES), lambda b, l: (row(b, l), 0)),
            pl.BlockSpec((1, hd), lambda b, l: (0, 0)),
        ],
        out_specs=pl.BlockSpec((lb, val), lambda b, l: (row(b, l), 0)),
        out_shape=jax.ShapeDtypeStruct((batch * seq, val), BF16),
        scratch_shapes=[pltpu.VMEM((qk_heads, hd, rep * hd), F32)],
        compiler_params=_params(("arbitrary", "arbitrary")),
        name="gdn_scan",
    )(proj, proj, proj, proj, gb, norm_w)


def _pad_lanes(a):
    return jnp.pad(a, ((0, 0), (0, LANES - a.shape[1])))


def kernel(x, gla_w_in, gla_w_gate_up, gla_b_gate, gla_norm_w, gla_w_out, gdn_w_in, gdn_conv_w,
           gdn_a_log, gdn_dt_bias, gdn_norm_w, gdn_w_out, mix_norm_w, ffn_norm_w, ffn_w_gate_up,
           ffn_w_down, final_norm_w):
    batch, seq, d = x.shape
    depth = mix_norm_w.shape[0]
    t = batch * seq

    gla_rank, gla_key = gla_w_gate_up.shape[1:]
    gla_dv = gla_norm_w.shape[1]
    gla_val = gla_w_out.shape[1]
    gla_heads = gla_val // gla_dv
    gla_dk = gla_key // gla_heads
    gla_main = 2 * gla_key + 2 * gla_val

    hd = gdn_norm_w.shape[1]
    gdn_v_heads = gdn_a_log.shape[1]
    gdn_val = gdn_w_out.shape[1]
    gdn_conv_ch = gdn_conv_w.shape[2]
    gdn_key = (gdn_conv_ch - gdn_val) // 2
    gdn_qk_heads = gdn_key // hd
    gdn_main = gdn_conv_ch + gdn_val

    tm = min(1024, seq)
    tn = 1024
    lb = min(256, seq)
    row = lambda a: a.reshape(1, -1)

    h = x.reshape(t, d)
    for i in range(depth):
        j = i // 2
        if i % 2 == 0:
            w_in = gla_w_in[j]
            proj, log_a = _gla_proj(
                h, row(mix_norm_w[i]), w_in[:, :gla_main].astype(BF16),
                w_in[:, gla_main:].astype(BF16), gla_w_gate_up[j], row(gla_b_gate[j]),
                tm=tm, tn=tn)
            o = _gla_scan(proj, log_a, row(gla_norm_w[j]), batch=batch, seq=seq,
                          heads=gla_heads, dk=gla_dk, dv=gla_dv, lb=lb)
            w_out = gla_w_out[j]
        else:
            w_in = gdn_w_in[j]
            rep = gdn_v_heads // gdn_qk_heads
            perm = jnp.array([p * rep + r for r in range(rep) for p in range(gdn_qk_heads)])
            w_b = w_in[:, gdn_main:gdn_main + gdn_v_heads][:, perm]
            w_a = w_in[:, gdn_main + gdn_v_heads:][:, perm]
            proj, gb = _gdn_proj(
                h, row(mix_norm_w[i]), w_in[:, :gdn_main].astype(BF16),
                _pad_lanes(jnp.concatenate([w_a, w_b], axis=1)).astype(BF16),
                _pad_lanes(row(gdn_a_log[j][perm])),
                _pad_lanes(row(gdn_dt_bias[j][perm])),
                gdn_conv_w[j], seq=seq, tm=tm, tn=tn, key=gdn_key, conv_ch=gdn_conv_ch,
                heads=gdn_v_heads, head_dim=hd)
            o = _gdn_scan(proj, gb, row(gdn_norm_w[j]), batch=batch, seq=seq,
                          qk_heads=gdn_qk_heads, rep=rep, hd=hd, lb=lb, group=8)
            w_out = gdn_w_out[j]
        h = _post(h, o, w_out.astype(BF16), row(ffn_norm_w[i]), ffn_w_gate_up[i].astype(BF16),
                  ffn_w_down[i].astype(BF16), row(final_norm_w), tm=min(512, seq),
                  final=(i == depth - 1))
    return h.reshape(batch, seq, d)
```

```python
import functools

import jax
import jax.numpy as jnp
from jax import lax
from jax.experimental import pallas as pl
from jax.experimental.pallas import tpu as pltpu

F32 = jnp.float32
BF16 = jnp.bfloat16
HIGHEST = lax.Precision.HIGHEST

NORM_EPS = 1e-6
L2_EPS = 1e-6
LANES = 128
CHUNK = 64
SUB = 16
GLA_TAU = 16.0
GDN_CONV = 4
CARRY_ROWS = 8
CONV_STRIDE = 4
CONV_ROWS = 8 * CONV_STRIDE
VMEM_LIMIT = 56 * 1024 * 1024


def _dot(a, b):
    return jnp.dot(a, b, preferred_element_type=F32)


def _dot_nt(a, b):
    return lax.dot_general(a, b, (((1,), (1,)), ((), ())), preferred_element_type=F32)


def _dot_tn(a, b):
    return lax.dot_general(a, b, (((0,), (0,)), ((), ())), preferred_element_type=F32)


def _dot_exact(a, b):
    return jnp.dot(a, b, precision=HIGHEST, preferred_element_type=F32)


def _rms(x, w):
    return x * lax.rsqrt(jnp.mean(x * x, axis=-1, keepdims=True) + NORM_EPS) * w


def _sigmoid(x):
    return 1.0 / (1.0 + jnp.exp(-x))


def _silu(x):
    return x * _sigmoid(x)


def _softplus(x):
    return jnp.maximum(x, 0.0) + jnp.log(1.0 + jnp.exp(-jnp.abs(x)))


def _tri_masks(n):
    row = lax.broadcasted_iota(jnp.int32, (n, n), 0)
    col = lax.broadcasted_iota(jnp.int32, (n, n), 1)
    return row >= col, row > col


def _params(sem):
    return pltpu.CompilerParams(dimension_semantics=sem, vmem_limit_bytes=VMEM_LIMIT)


def _const_spec(shape):
    return pl.BlockSpec(shape, lambda *_: (0,) * len(shape), pipeline_mode=pl.Buffered(1))


def _gla_proj_kernel(x_ref, nw_ref, w_ref, wg_ref, wgu_ref, bg_ref, proj_ref, la_ref, *, tn):
    xn = _rms(x_ref[...], nw_ref[...]).astype(BF16)
    g_low = _dot(xn, wg_ref[...])
    z = _dot(g_low.astype(BF16), wgu_ref[...]) + bg_ref[...]
    la_ref[...] = -_softplus(-z) / GLA_TAU
    for c in range(proj_ref.shape[1] // tn):
        cols = slice(c * tn, (c + 1) * tn)
        proj_ref[:, cols] = _dot(xn, w_ref[:, cols]).astype(proj_ref.dtype)


def _gla_proj(x2, norm_w, w_main, w_g, w_gu, b_g, *, tm, tn):
    t, d = x2.shape
    n = w_main.shape[1]
    rank = w_g.shape[1]
    key = w_gu.shape[1]
    return pl.pallas_call(
        functools.partial(_gla_proj_kernel, tn=tn),
        grid=(t // tm,),
        in_specs=[
            pl.BlockSpec((tm, d), lambda i: (i, 0)),
            _const_spec((1, d)),
            _const_spec((d, n)),
            _const_spec((d, rank)),
            _const_spec((rank, key)),
            _const_spec((1, key)),
        ],
        out_specs=[
            pl.BlockSpec((tm, n), lambda i: (i, 0)),
            pl.BlockSpec((tm, key), lambda i: (i, 0)),
        ],
        out_shape=[
            jax.ShapeDtypeStruct((t, n), BF16),
            jax.ShapeDtypeStruct((t, key), F32),
        ],
        compiler_params=_params(("arbitrary",)),
        name="gla_proj",
    )(x2, norm_w, w_main, w_g, w_gu, b_g)


def _gla_scan_kernel(q_ref, k_ref, v_ref, r_ref, la_ref, nw_ref, o_ref, st_ref, *,
                     nseq, heads, dk, dv, n_chunks):
    @pl.when(pl.program_id(1) == 0)
    def _():
        st_ref[...] = jnp.zeros_like(st_ref)

    causal, _ = _tri_masks(CHUNK)
    tril = causal.astype(F32)
    krow = lax.broadcasted_iota(jnp.int32, (CHUNK, dk), 0)
    scale = dk ** -0.5
    nw = nw_ref[...]
    sh = [(s, h) for s in range(nseq) for h in range(heads)]
    n = range(len(sh))

    def chunk_body(c, carry):
        rows = pl.ds(pl.multiple_of(c * CHUNK, CHUNK), CHUNK)
        g_all = [_dot_exact(tril, la_ref[s, rows, :]) for s in range(nseq)]
        g = [g_all[s][:, h * dk:(h + 1) * dk] for s, h in sh]
        q = [q_ref[s, rows, h * dk:(h + 1) * dk].astype(F32) * scale for s, h in sh]
        k = [k_ref[s, rows, h * dk:(h + 1) * dk].astype(F32) for s, h in sh]
        v = [v_ref[s, rows, h * dv:(h + 1) * dv] for s, h in sh]
        blocks = [[] for _ in n]
        for b in range(CHUNK // SUB):
            for i in n:
                ref_row = g[i][b * SUB:b * SUB + 1, :]
                qb = q[i][b * SUB:(b + 1) * SUB, :] * jnp.exp(g[i][b * SUB:(b + 1) * SUB, :] - ref_row)
                e = jnp.where(krow < (b + 1) * SUB, ref_row - g[i], 0.0)
                kb = k[i] * jnp.exp(e)
                blocks[i].append(_dot_nt(qb.astype(BF16), kb.astype(BF16)))
        scores = [jnp.where(causal, jnp.concatenate(blocks[i], axis=0), 0.0).astype(BF16) for i in n]
        st = [st_ref[s, h] for s, h in sh]
        o = [_dot(scores[i], v[i]) + _dot_nt((q[i] * jnp.exp(g[i])).astype(BF16), st[i].astype(BF16))
             for i in n]
        g_last = [g[i][CHUNK - 1:CHUNK, :] for i in n]
        k_dec = [(k[i] * jnp.exp(g_last[i] - g[i])).astype(BF16) for i in n]
        for i, (s, h) in enumerate(sh):
            st_ref[s, h] = st[i] * jnp.exp(g_last[i]) + _dot_tn(v[i], k_dec[i])
            y = _rms(o[i], nw) * _silu(r_ref[s, rows, h * dv:(h + 1) * dv].astype(F32))
            o_ref[s, rows, h * dv:(h + 1) * dv] = y.astype(o_ref.dtype)
        return carry

    lax.fori_loop(0, n_chunks, chunk_body, 0)


def _seq_groups(a, nseq):
    return a.reshape(nseq, a.shape[0] // nseq, a.shape[1])


def _gla_scan(proj, log_a, norm_w, *, batch, seq, heads, dk, dv, lb, nseq):
    key = heads * dk
    val = heads * dv
    nb = seq // lb
    kern = functools.partial(_gla_scan_kernel, nseq=nseq, heads=heads, dk=dk, dv=dv,
                             n_chunks=lb // CHUNK)
    proj3 = _seq_groups(proj, nseq)
    spec = lambda width, col: pl.BlockSpec((nseq, lb, width), lambda b, l: (0, b * nb + l, col))
    out = pl.pallas_call(
        kern,
        grid=(batch // nseq, nb),
        in_specs=[
            spec(key, 0),
            spec(key, 1),
            spec(val, 2 * key // val),
            spec(val, 2 * key // val + 1),
            spec(key, 0),
            _const_spec((1, dv)),
        ],
        out_specs=spec(val, 0),
        out_shape=jax.ShapeDtypeStruct((nseq, batch * seq // nseq, val), BF16),
        scratch_shapes=[pltpu.VMEM((nseq, heads, dv, dk), F32)],
        compiler_params=_params(("arbitrary", "arbitrary")),
        name="gla_scan",
    )(proj3, proj3, proj3, proj3, _seq_groups(log_a, nseq), norm_w)
    return out.reshape(batch * seq, val)


def _post_kernel(h_ref, o_ref, wout_ref, nw_ref, wgu_ref, wd_ref, fw_ref, out_ref, *, d_ff, final):
    h1 = h_ref[...] + _dot(o_ref[...], wout_ref[...])
    xn = _rms(h1, nw_ref[...]).astype(BF16)
    gu = _dot(xn, wgu_ref[...])
    act = (_silu(gu[:, :d_ff]) * gu[:, d_ff:]).astype(BF16)
    out = h1 + _dot(act, wd_ref[...])
    if final:
        out = _rms(out, fw_ref[...])
    out_ref[...] = out


def _post(h2, o2, w_out, norm_w, w_gu, w_d, final_w, *, tm, final):
    t, d = h2.shape
    val = o2.shape[1]
    d_ff = w_d.shape[0]
    return pl.pallas_call(
        functools.partial(_post_kernel, d_ff=d_ff, final=final),
        grid=(t // tm,),
        in_specs=[
            pl.BlockSpec((tm, d), lambda i: (i, 0)),
            pl.BlockSpec((tm, val), lambda i: (i, 0)),
            _const_spec((val, d)),
            _const_spec((1, d)),
            _const_spec((d, 2 * d_ff)),
            _const_spec((d_ff, d)),
            _const_spec((1, d)),
        ],
        out_specs=pl.BlockSpec((tm, d), lambda i: (i, 0)),
        out_shape=jax.ShapeDtypeStruct((t, d), F32),
        compiler_params=_params(("arbitrary",)),
        name="post_final" if final else "post",
    )(h2, o2, w_out, norm_w, w_gu, w_d, final_w)


def _spread_order(n_heavy, n_light):
    items = [((a + 0.5) / n_heavy, 0, a) for a in range(n_heavy)]
    items += [((b + 0.5) / n_light, 1, b) for b in range(n_light)]
    return [(kind, idx) for _, kind, idx in sorted(items)]


def _gdn_proj_kernel(x_ref, nw_ref, w_ref, wba_ref, alog_ref, dt_ref, cw_ref,
                     proj_ref, gb_ref, carry_ref, ybuf0_ref, ybuf1_ref, obuf0_ref, obuf1_ref, *,
                     tiles_per_seq, key, conv_ch, heads, head_dim, tn):
    tm = x_ref.shape[0]
    ybufs = (ybuf0_ref, ybuf1_ref)
    obufs = (obuf0_ref, obuf1_ref)

    @pl.when(pl.program_id(0) % tiles_per_seq == 0)
    def _():
        carry_ref[...] = jnp.zeros_like(carry_ref)

    xn = _rms(x_ref[...], nw_ref[...]).astype(BF16)
    ba = _dot(xn, wba_ref[...])
    lane = lax.broadcasted_iota(jnp.int32, ba.shape, 1)
    g = -jnp.exp(alog_ref[...]) * _softplus(ba + dt_ref[...])
    gb_ref[...] = jnp.where(lane < heads, g, jnp.where(lane < 2 * heads, _sigmoid(ba), 0.0))

    def matmul(c, slot):
        y = _dot(xn, w_ref[:, c * tn:(c + 1) * tn])
        for hh in range(tn // head_dim):
            ybufs[slot][hh, CARRY_ROWS:, :] = y[:, hh * head_dim:(hh + 1) * head_dim]

    def epilogue(c, slot):
        if c * tn >= conv_ch:
            for hh in range(tn // head_dim):
                lo = c * tn + hh * head_dim
                proj_ref[:, lo:lo + head_dim] = ybufs[slot][hh, CARRY_ROWS:, :].astype(proj_ref.dtype)
            return
        for hh in range(tn // head_dim):
            lo = c * tn + hh * head_dim
            ybufs[slot][hh, 0:CARRY_ROWS, :] = carry_ref[:, lo:lo + head_dim]
            carry_ref[:, lo:lo + head_dim] = ybufs[slot][hh, tm:tm + CARRY_ROWS, :]
            cw = cw_ref[:, lo:lo + head_dim]
            taps_w = [cw[GDN_CONV - 1 - s:GDN_CONV - s, :] for s in range(GDN_CONV)]
            for rb in range(tm // CONV_ROWS):
                r0 = rb * CONV_ROWS
                taps = {a: ybufs[slot][hh, pl.ds(CARRY_ROWS + r0 + a, 8, stride=CONV_STRIDE), :]
                        for a in range(1 - GDN_CONV, CONV_STRIDE)}
                for a in range(CONV_STRIDE):
                    acc = taps[a] * taps_w[0]
                    for s in range(1, GDN_CONV):
                        acc = acc + taps[a - s] * taps_w[s]
                    act = _silu(acc)
                    if lo < 2 * key:
                        inv = lax.rsqrt(jnp.sum(act * act, axis=-1, keepdims=True) + L2_EPS)
                        if lo < key:
                            inv = inv * head_dim ** -0.5
                        act = act * inv
                    obufs[slot][hh, pl.ds(r0 + a, 8, stride=CONV_STRIDE), :] = act
                proj_ref[r0:r0 + CONV_ROWS, lo:lo + head_dim] = (
                    obufs[slot][hh, r0:r0 + CONV_ROWS, :].astype(proj_ref.dtype))

    n_conv = conv_ch // tn
    n_plain = proj_ref.shape[1] // tn - n_conv
    order = [idx if kind == 0 else n_conv + idx for kind, idx in _spread_order(n_conv, n_plain)]
    matmul(order[0], 0)
    for pos, c in enumerate(order):
        if pos + 1 < len(order):
            matmul(order[pos + 1], (pos + 1) % 2)
        epilogue(c, pos % 2)


def _gdn_proj(x2, norm_w, w_main, w_ba, a_log, dt_bias, conv_w, *, seq, tm, tn, key, conv_ch,
              heads, head_dim):
    t, d = x2.shape
    n = w_main.shape[1]
    kern = functools.partial(_gdn_proj_kernel, tiles_per_seq=seq // tm, key=key, conv_ch=conv_ch,
                             heads=heads, head_dim=head_dim, tn=tn)
    return pl.pallas_call(
        kern,
        grid=(t // tm,),
        in_specs=[
            pl.BlockSpec((tm, d), lambda i: (i, 0)),
            _const_spec((1, d)),
            _const_spec((d, n)),
            _const_spec((d, LANES)),
            _const_spec((1, LANES)),
            _const_spec((1, LANES)),
            _const_spec((GDN_CONV, conv_ch)),
        ],
        out_specs=[
            pl.BlockSpec((tm, n), lambda i: (i, 0)),
            pl.BlockSpec((tm, LANES), lambda i: (i, 0)),
        ],
        out_shape=[
            jax.ShapeDtypeStruct((t, n), BF16),
            jax.ShapeDtypeStruct((t, LANES), F32),
        ],
        scratch_shapes=[pltpu.VMEM((CARRY_ROWS, conv_ch), F32),
                        pltpu.VMEM((tn // head_dim, tm + CARRY_ROWS, head_dim), F32),
                        pltpu.VMEM((tn // head_dim, tm + CARRY_ROWS, head_dim), F32),
                        pltpu.VMEM((tn // head_dim, tm, head_dim), F32),
                        pltpu.VMEM((tn // head_dim, tm, head_dim), F32)],
        compiler_params=_params(("arbitrary",)),
        name="gdn_proj",
    )(x2, norm_w, w_main, w_ba, a_log, dt_bias, conv_w)


def _gdn_scan_kernel(q_ref, k_ref, v_ref, z_ref, gb_ref, nw_ref, o_ref, s_ref, gc_ref, gr_ref, *,
                     nseq, qk_heads, rep, hd, n_chunks):
    v_heads = qk_heads * rep
    width = rep * CHUNK
    rr = range(rep)

    @pl.when(pl.program_id(1) == 0)
    def _():
        s_ref[...] = jnp.zeros_like(s_ref)

    causal, _ = _tri_masks(CHUNK)
    tril = causal.astype(F32)
    row = lax.broadcasted_iota(jnp.int32, (CHUNK, width), 0)
    lane = lax.broadcasted_iota(jnp.int32, (CHUNK, width), 1)
    blk = lane // CHUNK
    col = lane - blk * CHUNK
    causal_p = row >= col
    strict_p = row > col
    eye_p = jnp.where(row == col, 1.0, 0.0)
    nw = nw_ref[...]

    def per_head(cols):
        out = cols[rep - 1]
        for r in reversed(range(rep - 1)):
            out = jnp.where(blk == r, cols[r], out)
        return out

    def block_diag(x):
        return jnp.concatenate([jnp.where(blk == r, x, jnp.zeros_like(x)) for r in rr], axis=0)

    def block_rows(parts):
        return jnp.concatenate(
            [jnp.concatenate([parts[r] if r2 == r else jnp.zeros_like(parts[r2]) for r2 in rr], axis=1)
             for r in rr], axis=0)

    def stack_heads(x):
        return jnp.concatenate(
            [x if r == 0 else pltpu.roll(x, LANES - r * qk_heads, 1) for r in rr], axis=0)

    sp = [(s, p) for s in range(nseq) for p in range(qk_heads)]
    n = range(len(sp))

    def vcols(p, r):
        return slice((p * rep + r) * hd, (p * rep + r + 1) * hd)

    for s in range(nseq):
        for cc in range(n_chunks):
            crow = slice(cc * CHUNK, (cc + 1) * CHUNK)
            g_cum = _dot_exact(tril, gb_ref[s, crow, :])
            gc_ref[s, crow, :] = g_cum
            gr_ref[s, cc] = stack_heads(g_cum).T[:gr_ref.shape[2], :]

    def chunk_body(c, carry):
        rows = pl.ds(pl.multiple_of(c * CHUNK, CHUNK), CHUNK)
        gb = [gb_ref[s, rows, :] for s in range(nseq)]
        g_cum = [gc_ref[s, rows, :] for s in range(nseq)]
        g_rows = [gr_ref[s, c] for s in range(nseq)]
        k = [k_ref[s, rows, p * hd:(p + 1) * hd] for s, p in sp]
        q = [q_ref[s, rows, p * hd:(p + 1) * hd] for s, p in sp]
        kq = [_dot_nt(jnp.concatenate([k[i], q[i]], axis=0), jnp.concatenate([k[i]] * rep, axis=0))
              for i in n]
        g_c = [[jnp.broadcast_to(g_cum[s][:, r * qk_heads + p:r * qk_heads + p + 1], (CHUNK, hd))
                for r in rr] for s, p in sp]
        b_c = [[jnp.broadcast_to(gb[s][:, v_heads + r * qk_heads + p:v_heads + r * qk_heads + p + 1],
                                 (CHUNK, hd)) for r in rr] for s, p in sp]
        decay = [jnp.where(causal_p,
                           jnp.exp(jnp.where(causal_p, per_head(g_c[i]) - g_rows[s][p:p + 1, :], 0.0)), 0.0)
                 for i, (s, p) in enumerate(sp)]
        e_g = [[jnp.exp(g) for g in g_c[i]] for i in n]
        kf = [k[i].astype(F32) for i in n]
        qf = [q[i].astype(F32) for i in n]
        st = [s_ref[s, p] for s, p in sp]
        ks_qs = [[_dot(jnp.concatenate([kf[i] * (b_c[i][r] * e_g[i][r]), qf[i] * e_g[i][r]],
                                       axis=0).astype(BF16),
                       st[i][:, r * hd:(r + 1) * hd].astype(BF16)) for r in rr] for i in n]
        pm = [jnp.where(strict_p, -(kq[i][:CHUNK] * per_head(b_c[i]) * decay[i]), 0.0) for i in n]
        t = [eye_p + pm[i] for i in n]
        pmb = [x.astype(BF16) for x in pm]
        pm = [_dot(pmb[i], block_diag(pmb[i])) for i in n]
        span = 2
        while 2 * span < CHUNK:
            pmb = [x.astype(BF16) for x in pm]
            both = [_dot(pmb[i], jnp.concatenate([block_diag(t[i].astype(BF16)), block_diag(pmb[i])],
                                                 axis=1)) for i in n]
            t = [t[i] + both[i][:, :width] for i in n]
            pm = [both[i][:, width:] for i in n]
            span *= 2
        t = [(t[i] + _dot(pm[i].astype(BF16), block_diag(t[i].astype(BF16)))).astype(BF16) for i in n]
        x = [block_rows([(v_ref[s, rows, vcols(p, r)].astype(F32) * b_c[i][r]
                          - ks_qs[i][r][:CHUNK]).astype(BF16) for r in rr])
             for i, (s, p) in enumerate(sp)]
        v_new = [_dot(t[i], x[i]) for i in n]
        v_blk = [block_rows([v_new[i][:, r * hd:(r + 1) * hd].astype(BF16) for r in rr]) for i in n]
        attn = [jnp.where(causal_p, kq[i][CHUNK:] * decay[i], 0.0).astype(BF16) for i in n]
        o_intra = [_dot(attn[i], v_blk[i]) for i in n]
        g_last = [[g[CHUNK - 1:CHUNK, :] for g in g_c[i]] for i in n]
        k_dec = [jnp.concatenate([kf[i] * jnp.exp(g_last[i][r] - g_c[i][r]) for r in rr],
                                 axis=0).astype(BF16) for i in n]
        e_last = [jnp.concatenate([jnp.broadcast_to(jnp.exp(g), (1, hd)) for g in g_last[i]], axis=1)
                  for i in n]
        for i, (s, p) in enumerate(sp):
            s_ref[s, p] = st[i] * e_last[i] + _dot_tn(k_dec[i], v_blk[i])
            for r in rr:
                o = ks_qs[i][r][CHUNK:] + o_intra[i][:, r * hd:(r + 1) * hd]
                y = _rms(o, nw) * _silu(z_ref[s, rows, vcols(p, r)].astype(F32))
                o_ref[s, rows, vcols(p, r)] = y.astype(o_ref.dtype)
        return carry

    lax.fori_loop(0, n_chunks, chunk_body, 0)


def _gdn_scan(proj, gb, norm_w, *, batch, seq, qk_heads, rep, hd, lb, nseq):
    assert rep * CHUNK == LANES and hd == LANES
    key = qk_heads * hd
    val = key * rep
    nb = seq // lb
    kern = functools.partial(_gdn_scan_kernel, nseq=nseq, qk_heads=qk_heads, rep=rep, hd=hd,
                             n_chunks=lb // CHUNK)
    proj3 = _seq_groups(proj, nseq)
    spec = lambda width, col: pl.BlockSpec((nseq, lb, width), lambda b, l: (0, b * nb + l, col))
    out = pl.pallas_call(
        kern,
        grid=(batch // nseq, nb),
        in_specs=[
            spec(key, 0),
            spec(key, 1),
            spec(val, 2 * key // val),
            spec(val, 2 * key // val + 1),
            spec(LANES, 0),
            _const_spec((1, hd)),
        ],
        out_specs=spec(val, 0),
        out_shape=jax.ShapeDtypeStruct((nseq, batch * seq // nseq, val), BF16),
        scratch_shapes=[pltpu.VMEM((nseq, qk_heads, hd, rep * hd), F32),
                        pltpu.VMEM((nseq, lb, LANES), F32),
                        pltpu.VMEM((nseq, lb // CHUNK, -(-qk_heads // 8) * 8, rep * CHUNK), F32)],
        compiler_params=_params(("arbitrary", "arbitrary")),
        name="gdn_scan",
    )(proj3, proj3, proj3, proj3, _seq_groups(gb, nseq), norm_w)
    return out.reshape(batch * seq, val)


def _pad_lanes(a):
    return jnp.pad(a, ((0, 0), (0, LANES - a.shape[1])))


def kernel(x, gla_w_in, gla_w_gate_up, gla_b_gate, gla_norm_w, gla_w_out, gdn_w_in, gdn_conv_w,
           gdn_a_log, gdn_dt_bias, gdn_norm_w, gdn_w_out, mix_norm_w, ffn_norm_w, ffn_w_gate_up,
           ffn_w_down, final_norm_w):
    batch, seq, d = x.shape
    depth = mix_norm_w.shape[0]
    t = batch * seq

    gla_rank, gla_key = gla_w_gate_up.shape[1:]
    gla_dv = gla_norm_w.shape[1]
    gla_val = gla_w_out.shape[1]
    gla_heads = gla_val // gla_dv
    gla_dk = gla_key // gla_heads
    gla_main = 2 * gla_key + 2 * gla_val

    hd = gdn_norm_w.shape[1]
    gdn_v_heads = gdn_a_log.shape[1]
    gdn_val = gdn_w_out.shape[1]
    gdn_conv_ch = gdn_conv_w.shape[2]
    gdn_key = (gdn_conv_ch - gdn_val) // 2
    gdn_qk_heads = gdn_key // hd
    gdn_main = gdn_conv_ch + gdn_val

    tm = min(512, seq)
    lb = min(512, seq)
    nseq = 2 if batch % 2 == 0 else 1
    row = lambda a: a.reshape(1, -1)

    h = x.reshape(t, d)
    for i in range(depth):
        j = i // 2
        if i % 2 == 0:
            w_in = gla_w_in[j]
            proj, log_a = _gla_proj(
                h, row(mix_norm_w[i]), w_in[:, :gla_main].astype(BF16),
                w_in[:, gla_main:].astype(BF16), gla_w_gate_up[j].astype(BF16), row(gla_b_gate[j]),
                tm=tm, tn=1024)
            o = _gla_scan(proj, log_a, row(gla_norm_w[j]), batch=batch, seq=seq,
                          heads=gla_heads, dk=gla_dk, dv=gla_dv, lb=lb, nseq=nseq)
            w_out = gla_w_out[j]
        else:
            w_in = gdn_w_in[j]
            rep = gdn_v_heads // gdn_qk_heads
            perm = jnp.array([p * rep + r for r in range(rep) for p in range(gdn_qk_heads)])
            w_b = w_in[:, gdn_main:gdn_main + gdn_v_heads][:, perm]
            w_a = w_in[:, gdn_main + gdn_v_heads:][:, perm]
            proj, gb = _gdn_proj(
                h, row(mix_norm_w[i]), w_in[:, :gdn_main].astype(BF16),
                _pad_lanes(jnp.concatenate([w_a, w_b], axis=1)).astype(BF16),
                _pad_lanes(row(gdn_a_log[j][perm])),
                _pad_lanes(row(gdn_dt_bias[j][perm])),
                gdn_conv_w[j], seq=seq, tm=tm, tn=512, key=gdn_key, conv_ch=gdn_conv_ch,
                heads=gdn_v_heads, head_dim=hd)
            o = _gdn_scan(proj, gb, row(gdn_norm_w[j]), batch=batch, seq=seq,
                          qk_heads=gdn_qk_heads, rep=rep, hd=hd, lb=lb, nseq=nseq)
            w_out = gdn_w_out[j]
        h = _post(h, o, w_out.astype(BF16), row(ffn_norm_w[i]), ffn_w_gate_up[i].astype(BF16),
                  ffn_w_down[i].astype(BF16), row(final_norm_w), tm=min(512, seq),
                  final=(i == depth - 1))
    return h.reshape(batch, seq, d)
```

```python
import functools

import jax
import jax.numpy as jnp
from jax import lax
from jax.experimental import pallas as pl
from jax.experimental.pallas import tpu as pltpu

F32 = jnp.float32
BF16 = jnp.bfloat16

NORM_EPS = 1e-6
L2_EPS = 1e-6
LANES = 128
CHUNK = 64
SUB = 16
GLA_TAU = 16.0
GDN_CONV = 4
CARRY_ROWS = 8
CONV_STRIDE = 4
CONV_ROWS = 8 * CONV_STRIDE
PROJ_STAGES = 4
VMEM_LIMIT = 56 * 1024 * 1024


def _dot(a, b):
    return jnp.dot(a, b, preferred_element_type=F32)


def _dot_nt(a, b):
    return lax.dot_general(a, b, (((1,), (1,)), ((), ())), preferred_element_type=F32)


def _dot_tn(a, b):
    return lax.dot_general(a, b, (((0,), (0,)), ((), ())), preferred_element_type=F32)


def _chunk_cumsum(x):
    hi = x.astype(BF16)
    r1 = x - hi.astype(F32)
    mid = r1.astype(BF16)
    lo = (r1 - mid.astype(F32)).astype(BF16)
    row = lax.broadcasted_iota(jnp.int32, (CHUNK, 3 * CHUNK), 0)
    col = lax.broadcasted_iota(jnp.int32, (CHUNK, 3 * CHUNK), 1)
    tril3 = jnp.where(row >= col % CHUNK, 1.0, 0.0).astype(BF16)
    return _dot(tril3, jnp.concatenate([hi, mid, lo], axis=0))


def _rms(x, w):
    return x * lax.rsqrt(jnp.mean(x * x, axis=-1, keepdims=True) + NORM_EPS) * w


def _sigmoid(x):
    return 1.0 / (1.0 + jnp.exp(-x))


def _silu(x):
    half = 0.5 * x
    return half + half * jnp.tanh(half)


def _softplus(x):
    return jnp.maximum(x, 0.0) + jnp.log(1.0 + jnp.exp(-jnp.abs(x)))


def _tri_masks(n):
    row = lax.broadcasted_iota(jnp.int32, (n, n), 0)
    col = lax.broadcasted_iota(jnp.int32, (n, n), 1)
    return row >= col, row > col


def _params(sem):
    return pltpu.CompilerParams(dimension_semantics=sem, vmem_limit_bytes=VMEM_LIMIT)


def _const_spec(shape):
    return pl.BlockSpec(shape, lambda *_: (0,) * len(shape), pipeline_mode=pl.Buffered(1))


def _gla_proj_kernel(x_ref, nw_ref, w_ref, wg_ref, wgu_ref, bg_ref, proj_ref, la_ref, *, tn):
    xn = _rms(x_ref[...], nw_ref[...]).astype(BF16)
    g_low = _dot(xn, wg_ref[...])
    z = _dot(g_low.astype(BF16), wgu_ref[...]) + bg_ref[...]
    la_ref[...] = -_softplus(-z) / GLA_TAU
    for c in range(proj_ref.shape[1] // tn):
        cols = slice(c * tn, (c + 1) * tn)
        proj_ref[:, cols] = _dot(xn, w_ref[:, cols]).astype(proj_ref.dtype)


def _gla_proj(x2, norm_w, w_main, w_g, w_gu, b_g, *, tm, tn):
    t, d = x2.shape
    n = w_main.shape[1]
    rank = w_g.shape[1]
    key = w_gu.shape[1]
    return pl.pallas_call(
        functools.partial(_gla_proj_kernel, tn=tn),
        grid=(t // tm,),
        in_specs=[
            pl.BlockSpec((tm, d), lambda i: (i, 0)),
            _const_spec((1, d)),
            _const_spec((d, n)),
            _const_spec((d, rank)),
            _const_spec((rank, key)),
            _const_spec((1, key)),
        ],
        out_specs=[
            pl.BlockSpec((tm, n), lambda i: (i, 0)),
            pl.BlockSpec((tm, key), lambda i: (i, 0)),
        ],
        out_shape=[
            jax.ShapeDtypeStruct((t, n), BF16),
            jax.ShapeDtypeStruct((t, key), F32),
        ],
        compiler_params=_params(("arbitrary",)),
        name="gla_proj",
    )(x2, norm_w, w_main, w_g, w_gu, b_g)


def _gla_scan_kernel(q_ref, k_ref, v_ref, la_ref, o_ref, st_ref, gc_ref, *,
                     nseq, heads, dk, dv, n_chunks):
    @pl.when(pl.program_id(1) == 0)
    def _():
        st_ref[...] = jnp.zeros_like(st_ref)

    causal, _ = _tri_masks(CHUNK)
    krow = lax.broadcasted_iota(jnp.int32, (CHUNK, dk), 0)
    scale = dk ** -0.5
    sh = [(s, h) for s in range(nseq) for h in range(heads)]
    n = range(len(sh))

    for s in range(nseq):
        for cc in range(n_chunks):
            crow = slice(cc * CHUNK, (cc + 1) * CHUNK)
            gc_ref[s, crow, :] = _chunk_cumsum(la_ref[s, crow, :])

    def chunk_body(c, carry):
        rows = pl.ds(pl.multiple_of(c * CHUNK, CHUNK), CHUNK)
        g = [gc_ref[s, rows, h * dk:(h + 1) * dk] for s, h in sh]
        q = [q_ref[s, rows, h * dk:(h + 1) * dk].astype(F32) * scale for s, h in sh]
        k = [k_ref[s, rows, h * dk:(h + 1) * dk].astype(F32) for s, h in sh]
        v = [v_ref[s, rows, h * dv:(h + 1) * dv] for s, h in sh]
        blocks = [[] for _ in n]
        for b in range(CHUNK // SUB):
            for i in n:
                ref_row = g[i][b * SUB:b * SUB + 1, :]
                qb = q[i][b * SUB:(b + 1) * SUB, :] * jnp.exp(g[i][b * SUB:(b + 1) * SUB, :] - ref_row)
                e = jnp.where(krow < (b + 1) * SUB, ref_row - g[i], 0.0)
                kb = k[i] * jnp.exp(e)
                blocks[i].append(_dot_nt(qb.astype(BF16), kb.astype(BF16)))
        scores = [jnp.where(causal, jnp.concatenate(blocks[i], axis=0), 0.0).astype(BF16) for i in n]
        st = [st_ref[s, h] for s, h in sh]
        o = [_dot(scores[i], v[i]) + _dot_nt((q[i] * jnp.exp(g[i])).astype(BF16), st[i].astype(BF16))
             for i in n]
        g_last = [g[i][CHUNK - 1:CHUNK, :] for i in n]
        k_dec = [(k[i] * jnp.exp(g_last[i] - g[i])).astype(BF16) for i in n]
        for i, (s, h) in enumerate(sh):
            st_ref[s, h] = st[i] * jnp.exp(g_last[i]) + _dot_tn(v[i], k_dec[i])
            o_ref[s, rows, h * dv:(h + 1) * dv] = o[i].astype(o_ref.dtype)
        return carry

    lax.fori_loop(0, n_chunks, chunk_body, 0)


def _seq_groups(a, nseq):
    return a.reshape(nseq, a.shape[0] // nseq, a.shape[1])


def _gla_scan(proj, log_a, *, batch, seq, heads, dk, dv, lb, nseq):
    key = heads * dk
    val = heads * dv
    nb = seq // lb
    kern = functools.partial(_gla_scan_kernel, nseq=nseq, heads=heads, dk=dk, dv=dv,
                             n_chunks=lb // CHUNK)
    proj3 = _seq_groups(proj, nseq)
    spec = lambda width, col: pl.BlockSpec((nseq, lb, width), lambda b, l: (0, b * nb + l, col))
    out = pl.pallas_call(
        kern,
        grid=(batch // nseq, nb),
        in_specs=[
            spec(key, 0),
            spec(key, 1),
            spec(val, 2 * key // val),
            spec(key, 0),
        ],
        out_specs=spec(val, 0),
        out_shape=jax.ShapeDtypeStruct((nseq, batch * seq // nseq, val), BF16),
        scratch_shapes=[pltpu.VMEM((nseq, heads, dv, dk), F32), pltpu.VMEM((nseq, lb, key), F32)],
        compiler_params=_params(("arbitrary", "arbitrary")),
        name="gla_scan",
    )(proj3, proj3, proj3, _seq_groups(log_a, nseq))
    return out.reshape(batch * seq, val)


def _post_kernel(h_ref, o_ref, gate_ref, ow_ref, wout_ref, nw_ref, wgu_ref, wd_ref, fw_ref, out_ref, *,
                 d_ff, final):
    dv = ow_ref.shape[1]
    ow = ow_ref[...]
    gated = [(_rms(o_ref[:, lo:lo + dv].astype(F32), ow)
              * _silu(gate_ref[:, lo:lo + dv].astype(F32))).astype(BF16)
             for lo in range(0, o_ref.shape[1], dv)]
    h1 = h_ref[...] + _dot(jnp.concatenate(gated, axis=1), wout_ref[...])
    xn = _rms(h1, nw_ref[...]).astype(BF16)
    gu = _dot(xn, wgu_ref[...])
    act = (_silu(gu[:, :d_ff]) * gu[:, d_ff:]).astype(BF16)
    out = h1 + _dot(act, wd_ref[...])
    if final:
        out = _rms(out, fw_ref[...])
    out_ref[...] = out


def _post(h2, o2, proj, gate_block, o_norm_w, w_out, norm_w, w_gu, w_d, final_w, *, tm, final):
    t, d = h2.shape
    val = o2.shape[1]
    d_ff = w_d.shape[0]
    return pl.pallas_call(
        functools.partial(_post_kernel, d_ff=d_ff, final=final),
        grid=(t // tm,),
        in_specs=[
            pl.BlockSpec((tm, d), lambda i: (i, 0)),
            pl.BlockSpec((tm, val), lambda i: (i, 0)),
            pl.BlockSpec((tm, val), lambda i: (i, gate_block)),
            _const_spec((1, o_norm_w.shape[1])),
            _const_spec((val, d)),
            _const_spec((1, d)),
            _const_spec((d, 2 * d_ff)),
            _const_spec((d_ff, d)),
            _const_spec((1, d)),
        ],
        out_specs=pl.BlockSpec((tm, d), lambda i: (i, 0)),
        out_shape=jax.ShapeDtypeStruct((t, d), F32),
        compiler_params=_params(("arbitrary",)),
        name="post_final" if final else "post",
    )(h2, o2, proj, o_norm_w, w_out, norm_w, w_gu, w_d, final_w)


def _spread_order(n_heavy, n_light):
    items = [((a + 0.5) / n_heavy, 0, a) for a in range(n_heavy)]
    items += [((b + 0.5) / n_light, 1, b) for b in range(n_light)]
    return [(kind, idx) for _, kind, idx in sorted(items)]


def _gdn_proj_kernel(x_ref, nw_ref, w_ref, wba_ref, alog_ref, dt_ref, cw_ref,
                     proj_ref, gb_ref, carry_ref, *bufs,
                     tiles_per_seq, key, conv_ch, heads, head_dim, tn):
    tm = x_ref.shape[0]
    ybufs = bufs[:PROJ_STAGES]
    obufs = bufs[PROJ_STAGES:]

    @pl.when(pl.program_id(0) % tiles_per_seq == 0)
    def _():
        carry_ref[...] = jnp.zeros_like(carry_ref)

    xn = _rms(x_ref[...], nw_ref[...]).astype(BF16)
    ba = _dot(xn, wba_ref[...])
    lane = lax.broadcasted_iota(jnp.int32, ba.shape, 1)
    g = -jnp.exp(alog_ref[...]) * _softplus(ba + dt_ref[...])
    gb_ref[...] = jnp.where(lane < heads, g, jnp.where(lane < 2 * heads, _sigmoid(ba), 0.0))

    def matmul(c, slot):
        y = _dot(xn, w_ref[:, c * tn:(c + 1) * tn])
        for hh in range(tn // head_dim):
            ybufs[slot][hh, CARRY_ROWS:, :] = y[:, hh * head_dim:(hh + 1) * head_dim]

    def epilogue(c, slot):
        if c * tn >= conv_ch:
            for hh in range(tn // head_dim):
                lo = c * tn + hh * head_dim
                proj_ref[:, lo:lo + head_dim] = ybufs[slot][hh, CARRY_ROWS:, :].astype(proj_ref.dtype)
            return
        for hh in range(tn // head_dim):
            lo = c * tn + hh * head_dim
            ybufs[slot][hh, 0:CARRY_ROWS, :] = carry_ref[:, lo:lo + head_dim]
            carry_ref[:, lo:lo + head_dim] = ybufs[slot][hh, tm:tm + CARRY_ROWS, :]
            cw = cw_ref[:, lo:lo + head_dim]
            taps_w = [cw[GDN_CONV - 1 - s:GDN_CONV - s, :] for s in range(GDN_CONV)]
            for rb in range(tm // CONV_ROWS):
                r0 = rb * CONV_ROWS
                taps = {a: ybufs[slot][hh, pl.ds(CARRY_ROWS + r0 + a, 8, stride=CONV_STRIDE), :]
                        for a in range(1 - GDN_CONV, CONV_STRIDE)}
                for a in range(CONV_STRIDE):
                    acc = taps[a] * taps_w[0]
                    for s in range(1, GDN_CONV):
                        acc = acc + taps[a - s] * taps_w[s]
                    act = _silu(acc)
                    if lo < 2 * key:
                        inv = lax.rsqrt(jnp.sum(act * act, axis=-1, keepdims=True) + L2_EPS)
                        if lo < key:
                            inv = inv * head_dim ** -0.5
                        act = act * inv
                    obufs[slot][hh, pl.ds(r0 + a, 8, stride=CONV_STRIDE), :] = act
                proj_ref[r0:r0 + CONV_ROWS, lo:lo + head_dim] = (
                    obufs[slot][hh, r0:r0 + CONV_ROWS, :].astype(proj_ref.dtype))

    n_conv = conv_ch // tn
    n_plain = proj_ref.shape[1] // tn - n_conv
    order = [idx if kind == 0 else n_conv + idx for kind, idx in _spread_order(n_conv, n_plain)]
    ahead = PROJ_STAGES - 1
    for pos in range(min(ahead, len(order))):
        matmul(order[pos], pos % PROJ_STAGES)
    for pos, c in enumerate(order):
        if pos + ahead < len(order):
            matmul(order[pos + ahead], (pos + ahead) % PROJ_STAGES)
        epilogue(c, pos % PROJ_STAGES)


def _gdn_proj(x2, norm_w, w_main, w_ba, a_log, dt_bias, conv_w, *, seq, tm, tn, key, conv_ch,
              heads, head_dim):
    t, d = x2.shape
    n = w_main.shape[1]
    kern = functools.partial(_gdn_proj_kernel, tiles_per_seq=seq // tm, key=key, conv_ch=conv_ch,
                             heads=heads, head_dim=head_dim, tn=tn)
    return pl.pallas_call(
        kern,
        grid=(t // tm,),
        in_specs=[
            pl.BlockSpec((tm, d), lambda i: (i, 0)),
            _const_spec((1, d)),
            _const_spec((d, n)),
            _const_spec((d, LANES)),
            _const_spec((1, LANES)),
            _const_spec((1, LANES)),
            _const_spec((GDN_CONV, conv_ch)),
        ],
        out_specs=[
            pl.BlockSpec((tm, n), lambda i: (i, 0)),
            pl.BlockSpec((tm, LANES), lambda i: (i, 0)),
        ],
        out_shape=[
            jax.ShapeDtypeStruct((t, n), BF16),
            jax.ShapeDtypeStruct((t, LANES), F32),
        ],
        scratch_shapes=(
            [pltpu.VMEM((CARRY_ROWS, conv_ch), F32)]
            + [pltpu.VMEM((tn // head_dim, tm + CARRY_ROWS, head_dim), F32)] * PROJ_STAGES
            + [pltpu.VMEM((tn // head_dim, tm, head_dim), F32)] * PROJ_STAGES),
        compiler_params=_params(("arbitrary",)),
        name="gdn_proj",
    )(x2, norm_w, w_main, w_ba, a_log, dt_bias, conv_w)


def _gdn_scan_kernel(q_ref, k_ref, v_ref, gb_ref, o_ref, s_ref, gc_ref, gr_ref, *,
                     nseq, qk_heads, rep, hd, n_chunks):
    v_heads = qk_heads * rep
    width = rep * CHUNK
    rr = range(rep)

    @pl.when(pl.program_id(1) == 0)
    def _():
        s_ref[...] = jnp.zeros_like(s_ref)

    row = lax.broadcasted_iota(jnp.int32, (CHUNK, width), 0)
    lane = lax.broadcasted_iota(jnp.int32, (CHUNK, width), 1)
    blk = lane // CHUNK
    col = lane - blk * CHUNK
    causal_p = row >= col
    strict_p = row > col
    eye_p = jnp.where(row == col, 1.0, 0.0)

    def per_head(cols):
        out = cols[rep - 1]
        for r in reversed(range(rep - 1)):
            out = jnp.where(blk == r, cols[r], out)
        return out

    def block_diag(x):
        return jnp.concatenate([jnp.where(blk == r, x, jnp.zeros_like(x)) for r in rr], axis=0)

    def block_rows(parts):
        return jnp.concatenate(
            [jnp.concatenate([parts[r] if r2 == r else jnp.zeros_like(parts[r2]) for r2 in rr], axis=1)
             for r in rr], axis=0)

    def stack_heads(x):
        return jnp.concatenate(
            [x if r == 0 else pltpu.roll(x, LANES - r * qk_heads, 1) for r in rr], axis=0)

    sp = [(s, p) for s in range(nseq) for p in range(qk_heads)]
    n = range(len(sp))

    def vcols(p, r):
        return slice((p * rep + r) * hd, (p * rep + r + 1) * hd)

    for s in range(nseq):
        for cc in range(n_chunks):
            crow = slice(cc * CHUNK, (cc + 1) * CHUNK)
            g_cum = _chunk_cumsum(gb_ref[s, crow, :])
            gc_ref[s, crow, :] = g_cum
            gr_ref[s, cc] = stack_heads(g_cum).T[:gr_ref.shape[2], :]

    def chunk_body(c, carry):
        rows = pl.ds(pl.multiple_of(c * CHUNK, CHUNK), CHUNK)
        gb = [gb_ref[s, rows, :] for s in range(nseq)]
        g_cum = [gc_ref[s, rows, :] for s in range(nseq)]
        g_rows = [gr_ref[s, c] for s in range(nseq)]
        k = [k_ref[s, rows, p * hd:(p + 1) * hd] for s, p in sp]
        q = [q_ref[s, rows, p * hd:(p + 1) * hd] for s, p in sp]
        kq = [_dot_nt(jnp.concatenate([k[i], q[i]], axis=0), jnp.concatenate([k[i]] * rep, axis=0))
              for i in n]
        g_c = [[jnp.broadcast_to(g_cum[s][:, r * qk_heads + p:r * qk_heads + p + 1], (CHUNK, hd))
                for r in rr] for s, p in sp]
        b_c = [[jnp.broadcast_to(gb[s][:, v_heads + r * qk_heads + p:v_heads + r * qk_heads + p + 1],
                                 (CHUNK, hd)) for r in rr] for s, p in sp]
        decay = [jnp.where(causal_p,
                           jnp.exp(jnp.where(causal_p, per_head(g_c[i]) - g_rows[s][p:p + 1, :], 0.0)), 0.0)
                 for i, (s, p) in enumerate(sp)]
        e_g = [[jnp.exp(g) for g in g_c[i]] for i in n]
        kf = [k[i].astype(F32) for i in n]
        qf = [q[i].astype(F32) for i in n]
        st = [s_ref[s, p] for s, p in sp]
        ks_qs = [[_dot(jnp.concatenate([kf[i] * (b_c[i][r] * e_g[i][r]), qf[i] * e_g[i][r]],
                                       axis=0).astype(BF16),
                       st[i][:, r * hd:(r + 1) * hd].astype(BF16)) for r in rr] for i in n]
        pm = [jnp.where(strict_p, -(kq[i][:CHUNK] * per_head(b_c[i]) * decay[i]), 0.0) for i in n]
        t = [eye_p + pm[i] for i in n]
        pmb = [x.astype(BF16) for x in pm]
        pm = [_dot(pmb[i], block_diag(pmb[i])) for i in n]
        span = 2
        while 2 * span < CHUNK:
            pmb = [x.astype(BF16) for x in pm]
            both = [_dot(pmb[i], jnp.concatenate([block_diag(t[i].astype(BF16)), block_diag(pmb[i])],
                                                 axis=1)) for i in n]
            t = [t[i] + both[i][:, :width] for i in n]
            pm = [both[i][:, width:] for i in n]
            span *= 2
        t = [(t[i] + _dot(pm[i].astype(BF16), block_diag(t[i].astype(BF16)))).astype(BF16) for i in n]
        x = [block_rows([(v_ref[s, rows, vcols(p, r)].astype(F32) * b_c[i][r]
                          - ks_qs[i][r][:CHUNK]).astype(BF16) for r in rr])
             for i, (s, p) in enumerate(sp)]
        v_new = [_dot(t[i], x[i]) for i in n]
        v_blk = [block_rows([v_new[i][:, r * hd:(r + 1) * hd].astype(BF16) for r in rr]) for i in n]
        attn = [jnp.where(causal_p, kq[i][CHUNK:] * decay[i], 0.0).astype(BF16) for i in n]
        o_intra = [_dot(attn[i], v_blk[i]) for i in n]
        g_last = [[g[CHUNK - 1:CHUNK, :] for g in g_c[i]] for i in n]
        k_dec = [jnp.concatenate([kf[i] * jnp.exp(g_last[i][r] - g_c[i][r]) for r in rr],
                                 axis=0).astype(BF16) for i in n]
        e_last = [jnp.concatenate([jnp.broadcast_to(jnp.exp(g), (1, hd)) for g in g_last[i]], axis=1)
                  for i in n]
        for i, (s, p) in enumerate(sp):
            s_ref[s, p] = st[i] * e_last[i] + _dot_tn(k_dec[i], v_blk[i])
            for r in rr:
                o = ks_qs[i][r][CHUNK:] + o_intra[i][:, r * hd:(r + 1) * hd]
                o_ref[s, rows, vcols(p, r)] = o.astype(o_ref.dtype)
        return carry

    lax.fori_loop(0, n_chunks, chunk_body, 0)


def _gdn_scan(proj, gb, *, batch, seq, qk_heads, rep, hd, lb, nseq):
    assert rep * CHUNK == LANES and hd == LANES
    key = qk_heads * hd
    val = key * rep
    nb = seq // lb
    kern = functools.partial(_gdn_scan_kernel, nseq=nseq, qk_heads=qk_heads, rep=rep, hd=hd,
                             n_chunks=lb // CHUNK)
    proj3 = _seq_groups(proj, nseq)
    spec = lambda width, col: pl.BlockSpec((nseq, lb, width), lambda b, l: (0, b * nb + l, col))
    out = pl.pallas_call(
        kern,
        grid=(batch // nseq, nb),
        in_specs=[
            spec(key, 0),
            spec(key, 1),
            spec(val, 2 * key // val),
            spec(LANES, 0),
        ],
        out_specs=spec(val, 0),
        out_shape=jax.ShapeDtypeStruct((nseq, batch * seq // nseq, val), BF16),
        scratch_shapes=[pltpu.VMEM((nseq, qk_heads, hd, rep * hd), F32),
                        pltpu.VMEM((nseq, lb, LANES), F32),
                        pltpu.VMEM((nseq, lb // CHUNK, -(-qk_heads // 8) * 8, rep * CHUNK), F32)],
        compiler_params=_params(("arbitrary", "arbitrary")),
        name="gdn_scan",
    )(proj3, proj3, proj3, _seq_groups(gb, nseq))
    return out.reshape(batch * seq, val)


def _pad_lanes(a):
    return jnp.pad(a, ((0, 0), (0, LANES - a.shape[1])))


def kernel(x, gla_w_in, gla_w_gate_up, gla_b_gate, gla_norm_w, gla_w_out, gdn_w_in, gdn_conv_w,
           gdn_a_log, gdn_dt_bias, gdn_norm_w, gdn_w_out, mix_norm_w, ffn_norm_w, ffn_w_gate_up,
           ffn_w_down, final_norm_w):
    batch, seq, d = x.shape
    depth = mix_norm_w.shape[0]
    t = batch * seq

    gla_rank, gla_key = gla_w_gate_up.shape[1:]
    gla_dv = gla_norm_w.shape[1]
    gla_val = gla_w_out.shape[1]
    gla_heads = gla_val // gla_dv
    gla_dk = gla_key // gla_heads
    gla_main = 2 * gla_key + 2 * gla_val

    hd = gdn_norm_w.shape[1]
    gdn_v_heads = gdn_a_log.shape[1]
    gdn_val = gdn_w_out.shape[1]
    gdn_conv_ch = gdn_conv_w.shape[2]
    gdn_key = (gdn_conv_ch - gdn_val) // 2
    gdn_qk_heads = gdn_key // hd
    gdn_main = gdn_conv_ch + gdn_val

    tm = min(512, seq)
    lb = min(512, seq)
    nseq = 2 if batch % 2 == 0 else 1
    row = lambda a: a.reshape(1, -1)

    h = x.reshape(t, d)
    for i in range(depth):
        j = i // 2
        if i % 2 == 0:
            w_in = gla_w_in[j]
            proj, log_a = _gla_proj(
                h, row(mix_norm_w[i]), w_in[:, :gla_main].astype(BF16),
                w_in[:, gla_main:].astype(BF16), gla_w_gate_up[j].astype(BF16), row(gla_b_gate[j]),
                tm=tm, tn=1024)
            o = _gla_scan(proj, log_a, batch=batch, seq=seq,
                          heads=gla_heads, dk=gla_dk, dv=gla_dv, lb=lb, nseq=nseq)
            w_out, o_norm_w = gla_w_out[j], gla_norm_w[j]
            gate_block = 2 * gla_key // gla_val + 1
        else:
            w_in = gdn_w_in[j]
            rep = gdn_v_heads // gdn_qk_heads
            perm = jnp.array([p * rep + r for r in range(rep) for p in range(gdn_qk_heads)])
            w_b = w_in[:, gdn_main:gdn_main + gdn_v_heads][:, perm]
            w_a = w_in[:, gdn_main + gdn_v_heads:][:, perm]
            proj, gb = _gdn_proj(
                h, row(mix_norm_w[i]), w_in[:, :gdn_main].astype(BF16),
                _pad_lanes(jnp.concatenate([w_a, w_b], axis=1)).astype(BF16),
                _pad_lanes(row(gdn_a_log[j][perm])),
                _pad_lanes(row(gdn_dt_bias[j][perm])),
                gdn_conv_w[j], seq=seq, tm=tm, tn=256, key=gdn_key, conv_ch=gdn_conv_ch,
                heads=gdn_v_heads, head_dim=hd)
            o = _gdn_scan(proj, gb, batch=batch, seq=seq,
                          qk_heads=gdn_qk_heads, rep=rep, hd=hd, lb=lb, nseq=nseq)
            w_out, o_norm_w = gdn_w_out[j], gdn_norm_w[j]
            gate_block = 2 * gdn_key // gdn_val + 1
        h = _post(h, o, proj, gate_block, row(o_norm_w), w_out.astype(BF16), row(ffn_norm_w[i]),
                  ffn_w_gate_up[i].astype(BF16), ffn_w_down[i].astype(BF16), row(final_norm_w),
                  tm=min(512, seq), final=(i == depth - 1))
    return h.reshape(batch, seq, d)
```

```python
import functools

import jax
import jax.numpy as jnp
from jax import lax
from jax.experimental import pallas as pl
from jax.experimental.pallas import tpu as pltpu

F32 = jnp.float32
BF16 = jnp.bfloat16

NORM_EPS = 1e-6
L2_EPS = 1e-6
LANES = 128
CHUNK = 64
SUB = 16
GLA_TAU = 16.0
GDN_CONV = 4
CARRY_ROWS = 8
CONV_STRIDE = 4
CONV_ROWS = 8 * CONV_STRIDE
PROJ_STAGES = 4
VMEM_LIMIT = 56 * 1024 * 1024


def _dot(a, b):
    return jnp.dot(a, b, preferred_element_type=F32)


def _dot_nt(a, b):
    return lax.dot_general(a, b, (((1,), (1,)), ((), ())), preferred_element_type=F32)


def _dot_tn(a, b):
    return lax.dot_general(a, b, (((0,), (0,)), ((), ())), preferred_element_type=F32)


def _chunk_cumsum(x):
    hi = x.astype(BF16)
    r1 = x - hi.astype(F32)
    mid = r1.astype(BF16)
    lo = (r1 - mid.astype(F32)).astype(BF16)
    row = lax.broadcasted_iota(jnp.int32, (CHUNK, 3 * CHUNK), 0)
    col = lax.broadcasted_iota(jnp.int32, (CHUNK, 3 * CHUNK), 1)
    tril3 = jnp.where(row >= col % CHUNK, 1.0, 0.0).astype(BF16)
    return _dot(tril3, jnp.concatenate([hi, mid, lo], axis=0))


def _rms(x, w):
    return x * lax.rsqrt(jnp.mean(x * x, axis=-1, keepdims=True) + NORM_EPS) * w


def _sigmoid(x):
    return 1.0 / (1.0 + jnp.exp(-x))


def _silu(x):
    half = 0.5 * x
    return half + half * jnp.tanh(half)


def _softplus(x):
    return jnp.maximum(x, 0.0) + jnp.log(1.0 + jnp.exp(-jnp.abs(x)))


def _tri_masks(n):
    row = lax.broadcasted_iota(jnp.int32, (n, n), 0)
    col = lax.broadcasted_iota(jnp.int32, (n, n), 1)
    return row >= col, row > col


def _params(sem):
    return pltpu.CompilerParams(dimension_semantics=sem, vmem_limit_bytes=VMEM_LIMIT)


def _const_spec(shape):
    return pl.BlockSpec(shape, lambda *_: (0,) * len(shape), pipeline_mode=pl.Buffered(1))


def _gla_proj_kernel(x_ref, nw_ref, w_ref, wgu_ref, bg_ref, proj_ref, la_ref, *, tn):
    xn = _rms(x_ref[...], nw_ref[...]).astype(BF16)
    g_low = _dot(xn, w_ref[:, proj_ref.shape[1]:])
    z = _dot(g_low.astype(BF16), wgu_ref[...]) + bg_ref[...]
    la_ref[...] = -_softplus(-z) / GLA_TAU
    for c in range(proj_ref.shape[1] // tn):
        cols = slice(c * tn, (c + 1) * tn)
        proj_ref[:, cols] = _dot(xn, w_ref[:, cols]).astype(proj_ref.dtype)


def _gla_proj(x2, norm_w, w_in, w_gu, b_g, *, tm, tn):
    t, d = x2.shape
    rank, key = w_gu.shape
    n = w_in.shape[1] - rank
    return pl.pallas_call(
        functools.partial(_gla_proj_kernel, tn=tn),
        grid=(t // tm,),
        in_specs=[
            pl.BlockSpec((tm, d), lambda i: (i, 0)),
            _const_spec((1, d)),
            _const_spec((d, n + rank)),
            _const_spec((rank, key)),
            _const_spec((1, key)),
        ],
        out_specs=[
            pl.BlockSpec((tm, n), lambda i: (i, 0)),
            pl.BlockSpec((tm, key), lambda i: (i, 0)),
        ],
        out_shape=[
            jax.ShapeDtypeStruct((t, n), BF16),
            jax.ShapeDtypeStruct((t, key), F32),
        ],
        compiler_params=_params(("arbitrary",)),
        name="gla_proj",
    )(x2, norm_w, w_in, w_gu, b_g)


def _gla_scan_kernel(q_ref, k_ref, v_ref, la_ref, o_ref, st_ref, gc_ref, *,
                     nseq, heads, dk, dv, n_chunks):
    @pl.when(pl.program_id(1) == 0)
    def _():
        st_ref[...] = jnp.zeros_like(st_ref)

    causal, _ = _tri_masks(CHUNK)
    krow = lax.broadcasted_iota(jnp.int32, (CHUNK, dk), 0)
    scale = dk ** -0.5
    sh = [(s, h) for s in range(nseq) for h in range(heads)]
    n = range(len(sh))

    for s in range(nseq):
        for cc in range(n_chunks):
            crow = slice(cc * CHUNK, (cc + 1) * CHUNK)
            gc_ref[s, crow, :] = _chunk_cumsum(la_ref[s, crow, :])

    def chunk_body(c, carry):
        rows = pl.ds(pl.multiple_of(c * CHUNK, CHUNK), CHUNK)
        g = [gc_ref[s, rows, h * dk:(h + 1) * dk] for s, h in sh]
        q = [q_ref[s, rows, h * dk:(h + 1) * dk].astype(F32) * scale for s, h in sh]
        k = [k_ref[s, rows, h * dk:(h + 1) * dk].astype(F32) for s, h in sh]
        v = [v_ref[s, rows, h * dv:(h + 1) * dv] for s, h in sh]
        blocks = [[] for _ in n]
        for b in range(CHUNK // SUB):
            for i in n:
                ref_row = g[i][b * SUB:b * SUB + 1, :]
                qb = q[i][b * SUB:(b + 1) * SUB, :] * jnp.exp(g[i][b * SUB:(b + 1) * SUB, :] - ref_row)
                e = jnp.where(krow < (b + 1) * SUB, ref_row - g[i], 0.0)
                kb = k[i] * jnp.exp(e)
                blocks[i].append(_dot_nt(qb.astype(BF16), kb.astype(BF16)))
        scores = [jnp.where(causal, jnp.concatenate(blocks[i], axis=0), 0.0).astype(BF16) for i in n]
        st = [st_ref[s, h] for s, h in sh]
        o = [_dot(scores[i], v[i]) + _dot_nt((q[i] * jnp.exp(g[i])).astype(BF16), st[i].astype(BF16))
             for i in n]
        g_last = [g[i][CHUNK - 1:CHUNK, :] for i in n]
        k_dec = [(k[i] * jnp.exp(g_last[i] - g[i])).astype(BF16) for i in n]
        for i, (s, h) in enumerate(sh):
            st_ref[s, h] = st[i] * jnp.exp(g_last[i]) + _dot_tn(v[i], k_dec[i])
            o_ref[s, rows, h * dv:(h + 1) * dv] = o[i].astype(o_ref.dtype)
        return carry

    lax.fori_loop(0, n_chunks, chunk_body, 0)


def _seq_groups(a, nseq):
    return a.reshape(nseq, a.shape[0] // nseq, a.shape[1])


def _gla_scan(proj, log_a, *, batch, seq, heads, dk, dv, lb, nseq):
    key = heads * dk
    val = heads * dv
    nb = seq // lb
    kern = functools.partial(_gla_scan_kernel, nseq=nseq, heads=heads, dk=dk, dv=dv,
                             n_chunks=lb // CHUNK)
    proj3 = _seq_groups(proj, nseq)
    spec = lambda width, col: pl.BlockSpec((nseq, lb, width), lambda b, l: (0, b * nb + l, col))
    out = pl.pallas_call(
        kern,
        grid=(batch // nseq, nb),
        in_specs=[
            spec(key, 0),
            spec(key, 1),
            spec(val, 2 * key // val),
            spec(key, 0),
        ],
        out_specs=spec(val, 0),
        out_shape=jax.ShapeDtypeStruct((nseq, batch * seq // nseq, val), BF16),
        scratch_shapes=[pltpu.VMEM((nseq, heads, dv, dk), F32), pltpu.VMEM((nseq, lb, key), F32)],
        compiler_params=_params(("arbitrary", "arbitrary")),
        name="gla_scan",
    )(proj3, proj3, proj3, _seq_groups(log_a, nseq))
    return out.reshape(batch * seq, val)


def _post_kernel(h_ref, o_ref, gate_ref, ow_ref, wout_ref, nw_ref, wgu_ref, wd_ref, fw_ref, out_ref, *,
                 d_ff, final):
    dv = ow_ref.shape[1]
    ow = ow_ref[...]
    gated = [(_rms(o_ref[:, lo:lo + dv].astype(F32), ow)
              * _silu(gate_ref[:, lo:lo + dv].astype(F32))).astype(BF16)
             for lo in range(0, o_ref.shape[1], dv)]
    h1 = h_ref[...] + _dot(jnp.concatenate(gated, axis=1), wout_ref[...])
    xn = _rms(h1, nw_ref[...]).astype(BF16)
    gu = _dot(xn, wgu_ref[...])
    act = (_silu(gu[:, :d_ff]) * gu[:, d_ff:]).astype(BF16)
    out = h1 + _dot(act, wd_ref[...])
    if final:
        out = _rms(out, fw_ref[...])
    out_ref[...] = out


def _post(h2, o2, proj, gate_block, o_norm_w, w_out, norm_w, w_gu, w_d, final_w, *, tm, final):
    t, d = h2.shape
    val = o2.shape[1]
    d_ff = w_d.shape[0]
    return pl.pallas_call(
        functools.partial(_post_kernel, d_ff=d_ff, final=final),
        grid=(t // tm,),
        in_specs=[
            pl.BlockSpec((tm, d), lambda i: (i, 0)),
            pl.BlockSpec((tm, val), lambda i: (i, 0)),
            pl.BlockSpec((tm, val), lambda i: (i, gate_block)),
            _const_spec((1, o_norm_w.shape[1])),
            _const_spec((val, d)),
            _const_spec((1, d)),
            _const_spec((d, 2 * d_ff)),
            _const_spec((d_ff, d)),
            _const_spec((1, d)),
        ],
        out_specs=pl.BlockSpec((tm, d), lambda i: (i, 0)),
        out_shape=jax.ShapeDtypeStruct((t, d), F32),
        compiler_params=_params(("arbitrary",)),
        name="post_final" if final else "post",
    )(h2, o2, proj, o_norm_w, w_out, norm_w, w_gu, w_d, final_w)


def _spread_order(n_heavy, n_light):
    items = [((a + 0.5) / n_heavy, 0, a) for a in range(n_heavy)]
    items += [((b + 0.5) / n_light, 1, b) for b in range(n_light)]
    return [(kind, idx) for _, kind, idx in sorted(items)]


def _gdn_proj_kernel(x_ref, nw_ref, w_ref, wba_ref, alog_ref, dt_ref, cw_ref,
                     proj_ref, gb_ref, carry_ref, *bufs,
                     tiles_per_seq, key, conv_ch, heads, head_dim, tn):
    tm = x_ref.shape[0]
    ybufs = bufs[:PROJ_STAGES]
    obufs = bufs[PROJ_STAGES:]

    @pl.when(pl.program_id(0) % tiles_per_seq == 0)
    def _():
        carry_ref[...] = jnp.zeros_like(carry_ref)

    xn = _rms(x_ref[...], nw_ref[...]).astype(BF16)
    ba = _dot(xn, wba_ref[...])
    lane = lax.broadcasted_iota(jnp.int32, ba.shape, 1)
    g = -jnp.exp(alog_ref[...]) * _softplus(ba + dt_ref[...])
    gb_ref[...] = jnp.where(lane < heads, g, jnp.where(lane < 2 * heads, _sigmoid(ba), 0.0))

    def matmul(c, slot):
        y = _dot(xn, w_ref[:, c * tn:(c + 1) * tn])
        if c * tn >= conv_ch:
            proj_ref[:, c * tn:(c + 1) * tn] = y.astype(proj_ref.dtype)
            return
        for hh in range(tn // head_dim):
            ybufs[slot][hh, CARRY_ROWS:, :] = y[:, hh * head_dim:(hh + 1) * head_dim]

    def epilogue(c, slot):
        if c * tn >= conv_ch:
            return
        for hh in range(tn // head_dim):
            lo = c * tn + hh * head_dim
            ybufs[slot][hh, 0:CARRY_ROWS, :] = carry_ref[:, lo:lo + head_dim]
            carry_ref[:, lo:lo + head_dim] = ybufs[slot][hh, tm:tm + CARRY_ROWS, :]
            cw = cw_ref[:, lo:lo + head_dim]
            taps_w = [cw[GDN_CONV - 1 - s:GDN_CONV - s, :] for s in range(GDN_CONV)]
            for rb in range(tm // CONV_ROWS):
                r0 = rb * CONV_ROWS
                taps = {a: ybufs[slot][hh, pl.ds(CARRY_ROWS + r0 + a, 8, stride=CONV_STRIDE), :]
                        for a in range(1 - GDN_CONV, CONV_STRIDE)}
                for a in range(CONV_STRIDE):
                    acc = taps[a] * taps_w[0]
                    for s in range(1, GDN_CONV):
                        acc = acc + taps[a - s] * taps_w[s]
                    act = _silu(acc)
                    if lo < 2 * key:
                        inv = lax.rsqrt(jnp.sum(act * act, axis=-1, keepdims=True) + L2_EPS)
                        if lo < key:
                            inv = inv * head_dim ** -0.5
                        act = act * inv
                    obufs[slot][hh, pl.ds(r0 + a, 8, stride=CONV_STRIDE), :] = act
                proj_ref[r0:r0 + CONV_ROWS, lo:lo + head_dim] = (
                    obufs[slot][hh, r0:r0 + CONV_ROWS, :].astype(proj_ref.dtype))

    n_conv = conv_ch // tn
    n_plain = proj_ref.shape[1] // tn - n_conv
    order = [idx if kind == 0 else n_conv + idx for kind, idx in _spread_order(n_conv, n_plain)]
    ahead = PROJ_STAGES - 1
    for pos in range(min(ahead, len(order))):
        matmul(order[pos], pos % PROJ_STAGES)
    for pos, c in enumerate(order):
        if pos + ahead < len(order):
            matmul(order[pos + ahead], (pos + ahead) % PROJ_STAGES)
        epilogue(c, pos % PROJ_STAGES)


def _gdn_proj(x2, norm_w, w_in, w_ba, a_log, dt_bias, conv_w, *, seq, tm, tn, n, key, conv_ch,
              heads, head_dim):
    t, d = x2.shape
    kern = functools.partial(_gdn_proj_kernel, tiles_per_seq=seq // tm, key=key, conv_ch=conv_ch,
                             heads=heads, head_dim=head_dim, tn=tn)
    return pl.pallas_call(
        kern,
        grid=(t // tm,),
        in_specs=[
            pl.BlockSpec((tm, d), lambda i: (i, 0)),
            _const_spec((1, d)),
            _const_spec(w_in.shape),
            _const_spec((d, LANES)),
            _const_spec((1, LANES)),
            _const_spec((1, LANES)),
            _const_spec((GDN_CONV, conv_ch)),
        ],
        out_specs=[
            pl.BlockSpec((tm, n), lambda i: (i, 0)),
            pl.BlockSpec((tm, LANES), lambda i: (i, 0)),
        ],
        out_shape=[
            jax.ShapeDtypeStruct((t, n), BF16),
            jax.ShapeDtypeStruct((t, LANES), F32),
        ],
        scratch_shapes=(
            [pltpu.VMEM((CARRY_ROWS, conv_ch), F32)]
            + [pltpu.VMEM((tn // head_dim, tm + CARRY_ROWS, head_dim), F32)] * PROJ_STAGES
            + [pltpu.VMEM((tn // head_dim, tm, head_dim), F32)] * PROJ_STAGES),
        compiler_params=_params(("arbitrary",)),
        name="gdn_proj",
    )(x2, norm_w, w_in, w_ba, a_log, dt_bias, conv_w)


def _gdn_scan_kernel(q_ref, k_ref, v_ref, gb_ref, o_ref, s_ref, gc_ref, gr_ref, *,
                     nseq, qk_heads, rep, hd, n_chunks):
    v_heads = qk_heads * rep
    width = rep * CHUNK
    rr = range(rep)

    @pl.when(pl.program_id(1) == 0)
    def _():
        s_ref[...] = jnp.zeros_like(s_ref)

    row = lax.broadcasted_iota(jnp.int32, (CHUNK, width), 0)
    lane = lax.broadcasted_iota(jnp.int32, (CHUNK, width), 1)
    blk = lane // CHUNK
    col = lane - blk * CHUNK
    causal_p = row >= col
    strict_p = row > col
    eye_p = jnp.where(row == col, 1.0, 0.0)

    def per_head(cols):
        out = cols[rep - 1]
        for r in reversed(range(rep - 1)):
            out = jnp.where(blk == r, cols[r], out)
        return out

    def block_diag(x):
        return jnp.concatenate([jnp.where(blk == r, x, jnp.zeros_like(x)) for r in rr], axis=0)

    def block_rows(parts):
        return jnp.concatenate(
            [jnp.concatenate([parts[r] if r2 == r else jnp.zeros_like(parts[r2]) for r2 in rr], axis=1)
             for r in rr], axis=0)

    def stack_heads(x):
        return jnp.concatenate(
            [x if r == 0 else pltpu.roll(x, LANES - r * qk_heads, 1) for r in rr], axis=0)

    sp = [(s, p) for s in range(nseq) for p in range(qk_heads)]
    n = range(len(sp))

    def vcols(p, r):
        return slice((p * rep + r) * hd, (p * rep + r + 1) * hd)

    for s in range(nseq):
        for cc in range(n_chunks):
            crow = slice(cc * CHUNK, (cc + 1) * CHUNK)
            g_cum = _chunk_cumsum(gb_ref[s, crow, :])
            gc_ref[s, crow, :] = g_cum
            gr_ref[s, cc] = stack_heads(g_cum).T[:gr_ref.shape[2], :]

    def chunk_body(c, carry):
        rows = pl.ds(pl.multiple_of(c * CHUNK, CHUNK), CHUNK)
        gb = [gb_ref[s, rows, :] for s in range(nseq)]
        g_cum = [gc_ref[s, rows, :] for s in range(nseq)]
        g_rows = [gr_ref[s, c] for s in range(nseq)]
        k = [k_ref[s, rows, p * hd:(p + 1) * hd] for s, p in sp]
        q = [q_ref[s, rows, p * hd:(p + 1) * hd] for s, p in sp]
        kq = [_dot_nt(jnp.concatenate([k[i], q[i]], axis=0), jnp.concatenate([k[i]] * rep, axis=0))
              for i in n]
        g_c = [[jnp.broadcast_to(g_cum[s][:, r * qk_heads + p:r * qk_heads + p + 1], (CHUNK, hd))
                for r in rr] for s, p in sp]
        b_c = [[jnp.broadcast_to(gb[s][:, v_heads + r * qk_heads + p:v_heads + r * qk_heads + p + 1],
                                 (CHUNK, hd)) for r in rr] for s, p in sp]
        decay = [jnp.where(causal_p,
                           jnp.exp(jnp.where(causal_p, per_head(g_c[i]) - g_rows[s][p:p + 1, :], 0.0)), 0.0)
                 for i, (s, p) in enumerate(sp)]
        e_g = [[jnp.exp(g) for g in g_c[i]] for i in n]
        kf = [k[i].astype(F32) for i in n]
        st = [s_ref[s, p] for s, p in sp]
        kq_s = [_dot(jnp.concatenate([k[i], q[i]], axis=0), st[i].astype(BF16)) for i in n]
        ks = [[(b_c[i][r] * e_g[i][r]) * kq_s[i][:CHUNK, r * hd:(r + 1) * hd] for r in rr] for i in n]
        qs = [[e_g[i][r] * kq_s[i][CHUNK:, r * hd:(r + 1) * hd] for r in rr] for i in n]
        pm = [jnp.where(strict_p, -(kq[i][:CHUNK] * per_head(b_c[i]) * decay[i]), 0.0) for i in n]
        t = [eye_p + pm[i] for i in n]
        pmb = [x.astype(BF16) for x in pm]
        pm = [_dot(pmb[i], block_diag(pmb[i])) for i in n]
        span = 2
        while 2 * span < CHUNK:
            pmb = [x.astype(BF16) for x in pm]
            both = [_dot(pmb[i], jnp.concatenate([block_diag(t[i].astype(BF16)), block_diag(pmb[i])],
                                                 axis=1)) for i in n]
            t = [t[i] + both[i][:, :width] for i in n]
            pm = [both[i][:, width:] for i in n]
            span *= 2
        t = [(t[i] + _dot(pm[i].astype(BF16), block_diag(t[i].astype(BF16)))).astype(BF16) for i in n]
        x = [block_rows([(v_ref[s, rows, vcols(p, r)].astype(F32) * b_c[i][r]
                          - ks[i][r]).astype(BF16) for r in rr])
             for i, (s, p) in enumerate(sp)]
        v_new = [_dot(t[i], x[i]) for i in n]
        v_blk = [block_rows([v_new[i][:, r * hd:(r + 1) * hd].astype(BF16) for r in rr]) for i in n]
        attn = [jnp.where(causal_p, kq[i][CHUNK:] * decay[i], 0.0).astype(BF16) for i in n]
        o_intra = [_dot(attn[i], v_blk[i]) for i in n]
        g_last = [[g[CHUNK - 1:CHUNK, :] for g in g_c[i]] for i in n]
        k_dec = [jnp.concatenate([kf[i] * jnp.exp(g_last[i][r] - g_c[i][r]) for r in rr],
                                 axis=0).astype(BF16) for i in n]
        e_last = [jnp.concatenate([jnp.broadcast_to(jnp.exp(g), (1, hd)) for g in g_last[i]], axis=1)
                  for i in n]
        for i, (s, p) in enumerate(sp):
            s_ref[s, p] = st[i] * e_last[i] + _dot_tn(k_dec[i], v_blk[i])
            for r in rr:
                o = qs[i][r] + o_intra[i][:, r * hd:(r + 1) * hd]
                o_ref[s, rows, vcols(p, r)] = o.astype(o_ref.dtype)
        return carry

    lax.fori_loop(0, n_chunks, chunk_body, 0)


def _gdn_scan(proj, gb, *, batch, seq, qk_heads, rep, hd, lb, nseq):
    assert rep * CHUNK == LANES and hd == LANES
    key = qk_heads * hd
    val = key * rep
    nb = seq // lb
    kern = functools.partial(_gdn_scan_kernel, nseq=nseq, qk_heads=qk_heads, rep=rep, hd=hd,
                             n_chunks=lb // CHUNK)
    proj3 = _seq_groups(proj, nseq)
    spec = lambda width, col: pl.BlockSpec((nseq, lb, width), lambda b, l: (0, b * nb + l, col))
    out = pl.pallas_call(
        kern,
        grid=(batch // nseq, nb),
        in_specs=[
            spec(key, 0),
            spec(key, 1),
            spec(val, 2 * key // val),
            spec(LANES, 0),
        ],
        out_specs=spec(val, 0),
        out_shape=jax.ShapeDtypeStruct((nseq, batch * seq // nseq, val), BF16),
        scratch_shapes=[pltpu.VMEM((nseq, qk_heads, hd, rep * hd), F32),
                        pltpu.VMEM((nseq, lb, LANES), F32),
                        pltpu.VMEM((nseq, lb // CHUNK, -(-qk_heads // 8) * 8, rep * CHUNK), F32)],
        compiler_params=_params(("arbitrary", "arbitrary")),
        name="gdn_scan",
    )(proj3, proj3, proj3, _seq_groups(gb, nseq))
    return out.reshape(batch * seq, val)


def _pad_lanes(a):
    return jnp.pad(a, ((0, 0), (0, LANES - a.shape[1])))


def kernel(x, gla_w_in, gla_w_gate_up, gla_b_gate, gla_norm_w, gla_w_out, gdn_w_in, gdn_conv_w,
           gdn_a_log, gdn_dt_bias, gdn_norm_w, gdn_w_out, mix_norm_w, ffn_norm_w, ffn_w_gate_up,
           ffn_w_down, final_norm_w):
    batch, seq, d = x.shape
    depth = mix_norm_w.shape[0]
    t = batch * seq

    gla_rank, gla_key = gla_w_gate_up.shape[1:]
    gla_dv = gla_norm_w.shape[1]
    gla_val = gla_w_out.shape[1]
    gla_heads = gla_val // gla_dv
    gla_dk = gla_key // gla_heads
    gla_main = 2 * gla_key + 2 * gla_val

    hd = gdn_norm_w.shape[1]
    gdn_v_heads = gdn_a_log.shape[1]
    gdn_val = gdn_w_out.shape[1]
    gdn_conv_ch = gdn_conv_w.shape[2]
    gdn_key = (gdn_conv_ch - gdn_val) // 2
    gdn_qk_heads = gdn_key // hd
    gdn_main = gdn_conv_ch + gdn_val

    tm = min(512, seq)
    lb = min(512, seq)
    nseq = 2 if batch % 2 == 0 else 1
    row = lambda a: a.reshape(1, -1)

    h = x.reshape(t, d)
    for i in range(depth):
        j = i // 2
        if i % 2 == 0:
            proj, log_a = _gla_proj(
                h, row(mix_norm_w[i]), gla_w_in[j].astype(BF16), gla_w_gate_up[j].astype(BF16),
                row(gla_b_gate[j]), tm=min(1024, seq), tn=1024)
            o = _gla_scan(proj, log_a, batch=batch, seq=seq,
                          heads=gla_heads, dk=gla_dk, dv=gla_dv, lb=lb, nseq=nseq)
            w_out, o_norm_w = gla_w_out[j], gla_norm_w[j]
            gate_block = 2 * gla_key // gla_val + 1
        else:
            w_in = gdn_w_in[j]
            rep = gdn_v_heads // gdn_qk_heads
            perm = jnp.array([p * rep + r for r in range(rep) for p in range(gdn_qk_heads)])
            w_b = w_in[:, gdn_main:gdn_main + gdn_v_heads][:, perm]
            w_a = w_in[:, gdn_main + gdn_v_heads:][:, perm]
            proj, gb = _gdn_proj(
                h, row(mix_norm_w[i]), w_in.astype(BF16),
                _pad_lanes(jnp.concatenate([w_a, w_b], axis=1)).astype(BF16),
                _pad_lanes(row(gdn_a_log[j][perm])),
                _pad_lanes(row(gdn_dt_bias[j][perm])),
                gdn_conv_w[j], seq=seq, tm=tm, tn=256, n=gdn_main, key=gdn_key, conv_ch=gdn_conv_ch,
                heads=gdn_v_heads, head_dim=hd)
            o = _gdn_scan(proj, gb, batch=batch, seq=seq,
                          qk_heads=gdn_qk_heads, rep=rep, hd=hd, lb=lb, nseq=nseq)
            w_out, o_norm_w = gdn_w_out[j], gdn_norm_w[j]
            gate_block = 2 * gdn_key // gdn_val + 1
        h = _post(h, o, proj, gate_block, row(o_norm_w), w_out.astype(BF16), row(ffn_norm_w[i]),
                  ffn_w_gate_up[i].astype(BF16), ffn_w_down[i].astype(BF16), row(final_norm_w),
                  tm=min(512, seq), final=(i == depth - 1))
    return h.reshape(batch, seq, d)
```

```python
import functools

import jax
import jax.numpy as jnp
from jax import lax
from jax.experimental import pallas as pl
from jax.experimental.pallas import tpu as pltpu

F32 = jnp.float32
BF16 = jnp.bfloat16

NORM_EPS = 1e-6
L2_EPS = 1e-6
LANES = 128
CHUNK = 64
SUB = 16
GLA_TAU = 16.0
GDN_CONV = 4
CARRY_ROWS = 8
CONV_STRIDE = 4
CONV_ROWS = 8 * CONV_STRIDE
PROJ_STAGES = 4
VMEM_LIMIT = 56 * 1024 * 1024


def _dot(a, b):
    return jnp.dot(a, b, preferred_element_type=F32)


def _dot_nt(a, b):
    return lax.dot_general(a, b, (((1,), (1,)), ((), ())), preferred_element_type=F32)


def _dot_tn(a, b):
    return lax.dot_general(a, b, (((0,), (0,)), ((), ())), preferred_element_type=F32)


def _chunk_cumsum(x):
    hi = x.astype(BF16)
    r1 = x - hi.astype(F32)
    mid = r1.astype(BF16)
    lo = (r1 - mid.astype(F32)).astype(BF16)
    row = lax.broadcasted_iota(jnp.int32, (CHUNK, 3 * CHUNK), 0)
    col = lax.broadcasted_iota(jnp.int32, (CHUNK, 3 * CHUNK), 1)
    tril3 = jnp.where(row >= col % CHUNK, 1.0, 0.0).astype(BF16)
    return _dot(tril3, jnp.concatenate([hi, mid, lo], axis=0))


def _rms(x, w):
    return x * lax.rsqrt(jnp.mean(x * x, axis=-1, keepdims=True) + NORM_EPS) * w


def _sigmoid(x):
    return 1.0 / (1.0 + jnp.exp(-x))


def _silu(x):
    half = 0.5 * x
    return half + half * jnp.tanh(half)


def _softplus(x):
    return jnp.maximum(x, 0.0) + jnp.log(1.0 + jnp.exp(-jnp.abs(x)))


def _tri_masks(n):
    row = lax.broadcasted_iota(jnp.int32, (n, n), 0)
    col = lax.broadcasted_iota(jnp.int32, (n, n), 1)
    return row >= col, row > col


def _params(sem):
    return pltpu.CompilerParams(dimension_semantics=sem, vmem_limit_bytes=VMEM_LIMIT)


def _const_spec(shape):
    return pl.BlockSpec(shape, lambda *_: (0,) * len(shape), pipeline_mode=pl.Buffered(1))


def _gla_proj_kernel(x_ref, nw_ref, w_ref, wgu_ref, bg_ref, proj_ref, la_ref, *, tn):
    xn = _rms(x_ref[...], nw_ref[...]).astype(BF16)
    g_low = _dot(xn, w_ref[:, proj_ref.shape[1]:])
    z = _dot(g_low.astype(BF16), wgu_ref[...]) + bg_ref[...]
    la_ref[...] = -_softplus(-z) / GLA_TAU
    for c in range(proj_ref.shape[1] // tn):
        cols = slice(c * tn, (c + 1) * tn)
        proj_ref[:, cols] = _dot(xn, w_ref[:, cols]).astype(proj_ref.dtype)


def _gla_proj(x2, norm_w, w_in, w_gu, b_g, *, tm, tn):
    t, d = x2.shape
    rank, key = w_gu.shape
    n = w_in.shape[1] - rank
    return pl.pallas_call(
        functools.partial(_gla_proj_kernel, tn=tn),
        grid=(t // tm,),
        in_specs=[
            pl.BlockSpec((tm, d), lambda i: (i, 0)),
            _const_spec((1, d)),
            _const_spec((d, n + rank)),
            _const_spec((rank, key)),
            _const_spec((1, key)),
        ],
        out_specs=[
            pl.BlockSpec((tm, n), lambda i: (i, 0)),
            pl.BlockSpec((tm, key), lambda i: (i, 0)),
        ],
        out_shape=[
            jax.ShapeDtypeStruct((t, n), BF16),
            jax.ShapeDtypeStruct((t, key), F32),
        ],
        compiler_params=_params(("arbitrary",)),
        name="gla_proj",
    )(x2, norm_w, w_in, w_gu, b_g)


def _gla_scan_kernel(q_ref, k_ref, v_ref, la_ref, o_ref, st_ref, gc_ref, *,
                     nseq, heads, dk, dv, n_chunks):
    @pl.when(pl.program_id(1) == 0)
    def _():
        st_ref[...] = jnp.zeros_like(st_ref)

    causal, _ = _tri_masks(CHUNK)
    krow = lax.broadcasted_iota(jnp.int32, (CHUNK, dk), 0)
    scale = dk ** -0.5
    sh = [(s, h) for s in range(nseq) for h in range(heads)]
    n = range(len(sh))

    for s in range(nseq):
        for cc in range(n_chunks):
            crow = slice(cc * CHUNK, (cc + 1) * CHUNK)
            gc_ref[s, crow, :] = _chunk_cumsum(la_ref[s, crow, :])

    def chunk_body(c, carry):
        rows = pl.ds(pl.multiple_of(c * CHUNK, CHUNK), CHUNK)
        g = [gc_ref[s, rows, h * dk:(h + 1) * dk] for s, h in sh]
        q = [q_ref[s, rows, h * dk:(h + 1) * dk].astype(F32) * scale for s, h in sh]
        k = [k_ref[s, rows, h * dk:(h + 1) * dk].astype(F32) for s, h in sh]
        v = [v_ref[s, rows, h * dv:(h + 1) * dv] for s, h in sh]
        blocks = [[] for _ in n]
        for b in range(CHUNK // SUB):
            for i in n:
                ref_row = g[i][b * SUB:b * SUB + 1, :]
                qb = q[i][b * SUB:(b + 1) * SUB, :] * jnp.exp(g[i][b * SUB:(b + 1) * SUB, :] - ref_row)
                e = jnp.where(krow < (b + 1) * SUB, ref_row - g[i], 0.0)
                kb = k[i] * jnp.exp(e)
                blocks[i].append(_dot_nt(qb.astype(BF16), kb.astype(BF16)))
        scores = [jnp.where(causal, jnp.concatenate(blocks[i], axis=0), 0.0).astype(BF16) for i in n]
        st = [st_ref[s, h] for s, h in sh]
        o = [_dot(scores[i], v[i]) + _dot_nt((q[i] * jnp.exp(g[i])).astype(BF16), st[i].astype(BF16))
             for i in n]
        g_last = [g[i][CHUNK - 1:CHUNK, :] for i in n]
        k_dec = [(k[i] * jnp.exp(g_last[i] - g[i])).astype(BF16) for i in n]
        for i, (s, h) in enumerate(sh):
            st_ref[s, h] = st[i] * jnp.exp(g_last[i]) + _dot_tn(v[i], k_dec[i])
            o_ref[s, rows, h * dv:(h + 1) * dv] = o[i].astype(o_ref.dtype)
        return carry

    lax.fori_loop(0, n_chunks, chunk_body, 0)


def _seq_groups(a, nseq):
    return a.reshape(nseq, a.shape[0] // nseq, a.shape[1])


def _gla_scan(proj, log_a, *, batch, seq, heads, dk, dv, lb, nseq):
    key = heads * dk
    val = heads * dv
    nb = seq // lb
    kern = functools.partial(_gla_scan_kernel, nseq=nseq, heads=heads, dk=dk, dv=dv,
                             n_chunks=lb // CHUNK)
    proj3 = _seq_groups(proj, nseq)
    spec = lambda width, col: pl.BlockSpec((nseq, lb, width), lambda b, l: (0, b * nb + l, col))
    out = pl.pallas_call(
        kern,
        grid=(batch // nseq, nb),
        in_specs=[
            spec(key, 0),
            spec(key, 1),
            spec(val, 2 * key // val),
            spec(key, 0),
        ],
        out_specs=spec(val, 0),
        out_shape=jax.ShapeDtypeStruct((nseq, batch * seq // nseq, val), BF16),
        scratch_shapes=[pltpu.VMEM((nseq, heads, dv, dk), F32), pltpu.VMEM((nseq, lb, key), F32)],
        compiler_params=_params(("arbitrary", "arbitrary")),
        name="gla_scan",
    )(proj3, proj3, proj3, _seq_groups(log_a, nseq))
    return out.reshape(batch * seq, val)


def _post_kernel(h_ref, o_ref, gate_ref, ow_ref, wout_ref, nw_ref, wgu_ref, wd_ref, fw_ref, out_ref, *,
                 d_ff, final):
    dv = ow_ref.shape[1]
    ow = ow_ref[...]
    gated = [(_rms(o_ref[:, lo:lo + dv].astype(F32), ow)
              * _silu(gate_ref[:, lo:lo + dv].astype(F32))).astype(BF16)
             for lo in range(0, o_ref.shape[1], dv)]
    h1 = h_ref[...] + _dot(jnp.concatenate(gated, axis=1), wout_ref[...])
    xn = _rms(h1, nw_ref[...]).astype(BF16)
    gu = _dot(xn, wgu_ref[...])
    act = (_silu(gu[:, :d_ff]) * gu[:, d_ff:]).astype(BF16)
    out = h1 + _dot(act, wd_ref[...])
    if final:
        out = _rms(out, fw_ref[...])
    out_ref[...] = out


def _post(h2, o2, proj, gate_block, o_norm_w, w_out, norm_w, w_gu, w_d, final_w, *, layer, tm, final):
    t, d = h2.shape
    val = o2.shape[1]
    d_ff = w_d.shape[1]
    layer_spec = lambda shape: pl.BlockSpec((None,) + shape, lambda i: (layer, 0, 0),
                                            pipeline_mode=pl.Buffered(1))
    return pl.pallas_call(
        functools.partial(_post_kernel, d_ff=d_ff, final=final),
        grid=(t // tm,),
        in_specs=[
            pl.BlockSpec((tm, d), lambda i: (i, 0)),
            pl.BlockSpec((tm, val), lambda i: (i, 0)),
            pl.BlockSpec((tm, val), lambda i: (i, gate_block)),
            _const_spec((1, o_norm_w.shape[1])),
            _const_spec((val, d)),
            _const_spec((1, d)),
            layer_spec((d, 2 * d_ff)),
            layer_spec((d_ff, d)),
            _const_spec((1, d)),
        ],
        out_specs=pl.BlockSpec((tm, d), lambda i: (i, 0)),
        out_shape=jax.ShapeDtypeStruct((t, d), F32),
        compiler_params=_params(("arbitrary",)),
        name="post_final" if final else "post",
    )(h2, o2, proj, o_norm_w, w_out, norm_w, w_gu, w_d, final_w)


def _spread_order(n_heavy, n_light):
    items = [((a + 0.5) / n_heavy, 0, a) for a in range(n_heavy)]
    items += [((b + 0.5) / n_light, 1, b) for b in range(n_light)]
    return [(kind, idx) for _, kind, idx in sorted(items)]


def _gdn_proj_kernel(x_ref, nw_ref, w_ref, wba_ref, alog_ref, dt_ref, cw_ref,
                     proj_ref, gb_ref, carry_ref, *bufs,
                     tiles_per_seq, key, conv_ch, heads, head_dim, tn):
    tm = x_ref.shape[0]
    ybufs = bufs[:PROJ_STAGES]
    obufs = bufs[PROJ_STAGES:]

    @pl.when(pl.program_id(0) % tiles_per_seq == 0)
    def _():
        carry_ref[...] = jnp.zeros_like(carry_ref)

    xn = _rms(x_ref[...], nw_ref[...]).astype(BF16)
    ba = _dot(xn, wba_ref[...])
    lane = lax.broadcasted_iota(jnp.int32, ba.shape, 1)
    g = -jnp.exp(alog_ref[...]) * _softplus(ba + dt_ref[...])
    gb_ref[...] = jnp.where(lane < heads, g, jnp.where(lane < 2 * heads, _sigmoid(ba), 0.0))

    def matmul(c, slot):
        y = _dot(xn, w_ref[:, c * tn:(c + 1) * tn])
        if c * tn >= conv_ch:
            proj_ref[:, c * tn:(c + 1) * tn] = y.astype(proj_ref.dtype)
            return
        for hh in range(tn // head_dim):
            ybufs[slot][hh, CARRY_ROWS:, :] = y[:, hh * head_dim:(hh + 1) * head_dim]

    def epilogue(c, slot):
        if c * tn >= conv_ch:
            return
        for hh in range(tn // head_dim):
            lo = c * tn + hh * head_dim
            ybufs[slot][hh, 0:CARRY_ROWS, :] = carry_ref[:, lo:lo + head_dim]
            carry_ref[:, lo:lo + head_dim] = ybufs[slot][hh, tm:tm + CARRY_ROWS, :]
            cw = cw_ref[:, lo:lo + head_dim]
            taps_w = [cw[GDN_CONV - 1 - s:GDN_CONV - s, :] for s in range(GDN_CONV)]
            for rb in range(tm // CONV_ROWS):
                r0 = rb * CONV_ROWS
                taps = {a: ybufs[slot][hh, pl.ds(CARRY_ROWS + r0 + a, 8, stride=CONV_STRIDE), :]
                        for a in range(1 - GDN_CONV, CONV_STRIDE)}
                for a in range(CONV_STRIDE):
                    acc = taps[a] * taps_w[0]
                    for s in range(1, GDN_CONV):
                        acc = acc + taps[a - s] * taps_w[s]
                    act = _silu(acc)
                    if lo < 2 * key:
                        inv = lax.rsqrt(jnp.sum(act * act, axis=-1, keepdims=True) + L2_EPS)
                        if lo < key:
                            inv = inv * head_dim ** -0.5
                        act = act * inv
                    obufs[slot][hh, pl.ds(r0 + a, 8, stride=CONV_STRIDE), :] = act
                proj_ref[r0:r0 + CONV_ROWS, lo:lo + head_dim] = (
                    obufs[slot][hh, r0:r0 + CONV_ROWS, :].astype(proj_ref.dtype))

    n_conv = conv_ch // tn
    n_plain = proj_ref.shape[1] // tn - n_conv
    order = [idx if kind == 0 else n_conv + idx for kind, idx in _spread_order(n_conv, n_plain)]
    ahead = PROJ_STAGES - 1
    for pos in range(min(ahead, len(order))):
        matmul(order[pos], pos % PROJ_STAGES)
    for pos, c in enumerate(order):
        if pos + ahead < len(order):
            matmul(order[pos + ahead], (pos + ahead) % PROJ_STAGES)
        epilogue(c, pos % PROJ_STAGES)


def _gdn_proj(x2, norm_w, w_in, w_ba, a_log, dt_bias, conv_w, *, seq, tm, tn, n, key, conv_ch,
              heads, head_dim):
    t, d = x2.shape
    kern = functools.partial(_gdn_proj_kernel, tiles_per_seq=seq // tm, key=key, conv_ch=conv_ch,
                             heads=heads, head_dim=head_dim, tn=tn)
    return pl.pallas_call(
        kern,
        grid=(t // tm,),
        in_specs=[
            pl.BlockSpec((tm, d), lambda i: (i, 0)),
            _const_spec((1, d)),
            _const_spec(w_in.shape),
            _const_spec((d, LANES)),
            _const_spec((1, LANES)),
            _const_spec((1, LANES)),
            _const_spec((GDN_CONV, conv_ch)),
        ],
        out_specs=[
            pl.BlockSpec((tm, n), lambda i: (i, 0)),
            pl.BlockSpec((tm, LANES), lambda i: (i, 0)),
        ],
        out_shape=[
            jax.ShapeDtypeStruct((t, n), BF16),
            jax.ShapeDtypeStruct((t, LANES), F32),
        ],
        scratch_shapes=(
            [pltpu.VMEM((CARRY_ROWS, conv_ch), F32)]
            + [pltpu.VMEM((tn // head_dim, tm + CARRY_ROWS, head_dim), F32)] * PROJ_STAGES
            + [pltpu.VMEM((tn // head_dim, tm, head_dim), F32)] * PROJ_STAGES),
        compiler_params=_params(("arbitrary",)),
        name="gdn_proj",
    )(x2, norm_w, w_in, w_ba, a_log, dt_bias, conv_w)


def _gdn_scan_kernel(q_ref, k_ref, v_ref, gb_ref, o_ref, s_ref, gc_ref, gr_ref, *,
                     nseq, qk_heads, rep, hd, n_chunks):
    v_heads = qk_heads * rep
    width = rep * CHUNK
    rr = range(rep)

    @pl.when(pl.program_id(1) == 0)
    def _():
        s_ref[...] = jnp.zeros_like(s_ref)

    row = lax.broadcasted_iota(jnp.int32, (CHUNK, width), 0)
    lane = lax.broadcasted_iota(jnp.int32, (CHUNK, width), 1)
    blk = lane // CHUNK
    col = lane - blk * CHUNK
    causal_p = row >= col
    strict_p = row > col
    eye_p = jnp.where(row == col, 1.0, 0.0)

    def per_head(cols):
        out = cols[rep - 1]
        for r in reversed(range(rep - 1)):
            out = jnp.where(blk == r, cols[r], out)
        return out

    def block_diag(x):
        return jnp.concatenate([jnp.where(blk == r, x, jnp.zeros_like(x)) for r in rr], axis=0)

    def block_rows(parts):
        return jnp.concatenate(
            [jnp.concatenate([parts[r] if r2 == r else jnp.zeros_like(parts[r2]) for r2 in rr], axis=1)
             for r in rr], axis=0)

    def stack_heads(x):
        return jnp.concatenate(
            [x if r == 0 else pltpu.roll(x, LANES - r * qk_heads, 1) for r in rr], axis=0)

    sp = [(s, p) for s in range(nseq) for p in range(qk_heads)]
    n = range(len(sp))

    def vcols(p, r):
        return slice((p * rep + r) * hd, (p * rep + r + 1) * hd)

    for s in range(nseq):
        for cc in range(n_chunks):
            crow = slice(cc * CHUNK, (cc + 1) * CHUNK)
            g_cum = _chunk_cumsum(gb_ref[s, crow, :])
            gc_ref[s, crow, :] = g_cum
            gr_ref[s, cc] = stack_heads(g_cum).T[:gr_ref.shape[2], :]

    def chunk_body(c, carry):
        rows = pl.ds(pl.multiple_of(c * CHUNK, CHUNK), CHUNK)
        gb = [gb_ref[s, rows, :] for s in range(nseq)]
        g_cum = [gc_ref[s, rows, :] for s in range(nseq)]
        g_rows = [gr_ref[s, c] for s in range(nseq)]
        k = [k_ref[s, rows, p * hd:(p + 1) * hd] for s, p in sp]
        q = [q_ref[s, rows, p * hd:(p + 1) * hd] for s, p in sp]
        kq = [_dot_nt(jnp.concatenate([k[i], q[i]], axis=0), jnp.concatenate([k[i]] * rep, axis=0))
              for i in n]
        g_c = [[jnp.broadcast_to(g_cum[s][:, r * qk_heads + p:r * qk_heads + p + 1], (CHUNK, hd))
                for r in rr] for s, p in sp]
        b_c = [[jnp.broadcast_to(gb[s][:, v_heads + r * qk_heads + p:v_heads + r * qk_heads + p + 1],
                                 (CHUNK, hd)) for r in rr] for s, p in sp]
        decay = [jnp.where(causal_p,
                           jnp.exp(jnp.where(causal_p, per_head(g_c[i]) - g_rows[s][p:p + 1, :], 0.0)), 0.0)
                 for i, (s, p) in enumerate(sp)]
        e_g = [[jnp.exp(g) for g in g_c[i]] for i in n]
        kf = [k[i].astype(F32) for i in n]
        qf = [q[i].astype(F32) for i in n]
        st = [s_ref[s, p] for s, p in sp]
        ks_qs = [[_dot(jnp.concatenate([kf[i] * (b_c[i][r] * e_g[i][r]), qf[i] * e_g[i][r]],
                                       axis=0).astype(BF16),
                       st[i][:, r * hd:(r + 1) * hd].astype(BF16)) for r in rr] for i in n]
        ks = [[ks_qs[i][r][:CHUNK] for r in rr] for i in n]
        qs = [[ks_qs[i][r][CHUNK:] for r in rr] for i in n]
        pm = [jnp.where(strict_p, -(kq[i][:CHUNK] * per_head(b_c[i]) * decay[i]), 0.0) for i in n]
        t = [eye_p + pm[i] for i in n]
        pmb = [x.astype(BF16) for x in pm]
        pm = [_dot(pmb[i], block_diag(pmb[i])) for i in n]
        span = 2
        while 2 * span < CHUNK:
            pmb = [x.astype(BF16) for x in pm]
            both = [_dot(pmb[i], jnp.concatenate([block_diag(t[i].astype(BF16)), block_diag(pmb[i])],
                                                 axis=1)) for i in n]
            t = [t[i] + both[i][:, :width] for i in n]
            pm = [both[i][:, width:] for i in n]
            span *= 2
        t = [(t[i] + _dot(pm[i].astype(BF16), block_diag(t[i].astype(BF16)))).astype(BF16) for i in n]
        x = [block_rows([(v_ref[s, rows, vcols(p, r)].astype(F32) * b_c[i][r]
                          - ks[i][r]).astype(BF16) for r in rr])
             for i, (s, p) in enumerate(sp)]
        v_new = [_dot(t[i], x[i]) for i in n]
        v_blk = [block_rows([v_new[i][:, r * hd:(r + 1) * hd].astype(BF16) for r in rr]) for i in n]
        attn = [jnp.where(causal_p, kq[i][CHUNK:] * decay[i], 0.0).astype(BF16) for i in n]
        o_intra = [_dot(attn[i], v_blk[i]) for i in n]
        g_last = [[g[CHUNK - 1:CHUNK, :] for g in g_c[i]] for i in n]
        k_dec = [jnp.concatenate([kf[i] * jnp.exp(g_last[i][r] - g_c[i][r]) for r in rr],
                                 axis=0).astype(BF16) for i in n]
        e_last = [jnp.concatenate([jnp.broadcast_to(jnp.exp(g), (1, hd)) for g in g_last[i]], axis=1)
                  for i in n]
        for i, (s, p) in enumerate(sp):
            s_ref[s, p] = st[i] * e_last[i] + _dot_tn(k_dec[i], v_blk[i])
            for r in rr:
                o = qs[i][r] + o_intra[i][:, r * hd:(r + 1) * hd]
                o_ref[s, rows, vcols(p, r)] = o.astype(o_ref.dtype)
        return carry

    lax.fori_loop(0, n_chunks, chunk_body, 0)


def _gdn_scan(proj, gb, *, batch, seq, qk_heads, rep, hd, lb, nseq):
    assert rep * CHUNK == LANES and hd == LANES
    key = qk_heads * hd
    val = key * rep
    nb = seq // lb
    kern = functools.partial(_gdn_scan_kernel, nseq=nseq, qk_heads=qk_heads, rep=rep, hd=hd,
                             n_chunks=lb // CHUNK)
    proj3 = _seq_groups(proj, nseq)
    spec = lambda width, col: pl.BlockSpec((nseq, lb, width), lambda b, l: (0, b * nb + l, col))
    out = pl.pallas_call(
        kern,
        grid=(batch // nseq, nb),
        in_specs=[
            spec(key, 0),
            spec(key, 1),
            spec(val, 2 * key // val),
            spec(LANES, 0),
        ],
        out_specs=spec(val, 0),
        out_shape=jax.ShapeDtypeStruct((nseq, batch * seq // nseq, val), BF16),
        scratch_shapes=[pltpu.VMEM((nseq, qk_heads, hd, rep * hd), F32),
                        pltpu.VMEM((nseq, lb, LANES), F32),
                        pltpu.VMEM((nseq, lb // CHUNK, -(-qk_heads // 8) * 8, rep * CHUNK), F32)],
        compiler_params=_params(("arbitrary", "arbitrary")),
        name="gdn_scan",
    )(proj3, proj3, proj3, _seq_groups(gb, nseq))
    return out.reshape(batch * seq, val)


def _pad_lanes(a):
    return jnp.pad(a, ((0, 0), (0, LANES - a.shape[1])))


def kernel(x, gla_w_in, gla_w_gate_up, gla_b_gate, gla_norm_w, gla_w_out, gdn_w_in, gdn_conv_w,
           gdn_a_log, gdn_dt_bias, gdn_norm_w, gdn_w_out, mix_norm_w, ffn_norm_w, ffn_w_gate_up,
           ffn_w_down, final_norm_w):
    batch, seq, d = x.shape
    depth = mix_norm_w.shape[0]
    t = batch * seq

    gla_rank, gla_key = gla_w_gate_up.shape[1:]
    gla_dv = gla_norm_w.shape[1]
    gla_val = gla_w_out.shape[1]
    gla_heads = gla_val // gla_dv
    gla_dk = gla_key // gla_heads

    hd = gdn_norm_w.shape[1]
    gdn_v_heads = gdn_a_log.shape[1]
    gdn_val = gdn_w_out.shape[1]
    gdn_conv_ch = gdn_conv_w.shape[2]
    gdn_key = (gdn_conv_ch - gdn_val) // 2
    gdn_qk_heads = gdn_key // hd
    gdn_main = gdn_conv_ch + gdn_val

    tm = min(512, seq)
    lb = min(512, seq)
    nseq = 2 if batch % 2 == 0 else 1
    row = lambda a: a.reshape(1, -1)
    ffn_gu = ffn_w_gate_up.astype(BF16)
    ffn_d = ffn_w_down.astype(BF16)

    h = x.reshape(t, d)
    for i in range(depth):
        j = i // 2
        if i % 2 == 0:
            proj, log_a = _gla_proj(
                h, row(mix_norm_w[i]), gla_w_in[j].astype(BF16), gla_w_gate_up[j].astype(BF16),
                row(gla_b_gate[j]), tm=min(1024, seq), tn=1024)
            o = _gla_scan(proj, log_a, batch=batch, seq=seq,
                          heads=gla_heads, dk=gla_dk, dv=gla_dv, lb=lb,
                          nseq=4 if batch % 4 == 0 else nseq)
            w_out, o_norm_w = gla_w_out[j], gla_norm_w[j]
            gate_block = 2 * gla_key // gla_val + 1
        else:
            w_in = gdn_w_in[j]
            rep = gdn_v_heads // gdn_qk_heads
            perm = jnp.array([p * rep + r for r in range(rep) for p in range(gdn_qk_heads)])
            w_b = w_in[:, gdn_main:gdn_main + gdn_v_heads][:, perm]
            w_a = w_in[:, gdn_main + gdn_v_heads:][:, perm]
            proj, gb = _gdn_proj(
                h, row(mix_norm_w[i]), w_in.astype(BF16),
                _pad_lanes(jnp.concatenate([w_a, w_b], axis=1)).astype(BF16),
                _pad_lanes(row(gdn_a_log[j][perm])),
                _pad_lanes(row(gdn_dt_bias[j][perm])),
                gdn_conv_w[j], seq=seq, tm=tm, tn=256, n=gdn_main, key=gdn_key, conv_ch=gdn_conv_ch,
                heads=gdn_v_heads, head_dim=hd)
            o = _gdn_scan(proj, gb, batch=batch, seq=seq,
                          qk_heads=gdn_qk_heads, rep=rep, hd=hd, lb=lb, nseq=nseq)
            w_out, o_norm_w = gdn_w_out[j], gdn_norm_w[j]
            gate_block = 2 * gdn_key // gdn_val + 1
        h = _post(h, o, proj, gate_block, row(o_norm_w), w_out.astype(BF16), row(ffn_norm_w[i]),
                  ffn_gu, ffn_d, row(final_norm_w), layer=i, tm=min(512, seq), final=(i == depth - 1))
    return h.reshape(batch, seq, d)
```

```python
import functools

import jax
import jax.numpy as jnp
from jax import lax
from jax.experimental import pallas as pl
from jax.experimental.pallas import tpu as pltpu

F32 = jnp.float32
BF16 = jnp.bfloat16

NORM_EPS = 1e-6
L2_EPS = 1e-6
LANES = 128
SUBLANES = 8
MXU_COLS = 256
VMEM_BYTES = 64 * 1024 * 1024
VMEM_LIMIT = VMEM_BYTES * 7 // 8
CHUNK = 64
SUB = 16
GLA_TAU = 16.0
GDN_CONV = 4
CARRY_ROWS = SUBLANES
CONV_STRIDE = 4
CONV_ROWS = SUBLANES * CONV_STRIDE
PROJ_STAGES = 2


def _tiles(batch, seq):
    return dict(
        tm_gla_proj=min(1024, seq),
        tm=min(512, seq),
        tn_gla_proj=4 * MXU_COLS,
        tn_gdn_proj=MXU_COLS,
        lb=min(512, seq),
        nseq_gla=next(n for n in (4, 2, 1) if batch % n == 0),
        nseq_gdn=next(n for n in (2, 1) if batch % n == 0),
    )


def _dot(a, b):
    return jnp.dot(a, b, preferred_element_type=F32)


def _dot_nt(a, b):
    return lax.dot_general(a, b, (((1,), (1,)), ((), ())), preferred_element_type=F32)


def _dot_tn(a, b):
    return lax.dot_general(a, b, (((0,), (0,)), ((), ())), preferred_element_type=F32)


def _chunk_cumsum(x):
    hi = x.astype(BF16)
    r1 = x - hi.astype(F32)
    mid = r1.astype(BF16)
    lo = (r1 - mid.astype(F32)).astype(BF16)
    row = lax.broadcasted_iota(jnp.int32, (CHUNK, 3 * CHUNK), 0)
    col = lax.broadcasted_iota(jnp.int32, (CHUNK, 3 * CHUNK), 1)
    tril3 = jnp.where(row >= col % CHUNK, 1.0, 0.0).astype(BF16)
    return _dot(tril3, jnp.concatenate([hi, mid, lo], axis=0))


def _rms(x, w):
    return x * lax.rsqrt(jnp.mean(x * x, axis=-1, keepdims=True) + NORM_EPS) * w


def _sigmoid(x):
    return 1.0 / (1.0 + jnp.exp(-x))


def _silu(x):
    half = 0.5 * x
    return half + half * jnp.tanh(half)


def _softplus(x):
    return jnp.maximum(x, 0.0) + jnp.log(1.0 + jnp.exp(-jnp.abs(x)))


def _tri_masks(n):
    row = lax.broadcasted_iota(jnp.int32, (n, n), 0)
    col = lax.broadcasted_iota(jnp.int32, (n, n), 1)
    return row >= col, row > col


def _params(sem):
    return pltpu.CompilerParams(dimension_semantics=sem, vmem_limit_bytes=VMEM_LIMIT)


def _const_spec(shape):
    return pl.BlockSpec(shape, lambda *_: (0,) * len(shape), pipeline_mode=pl.Buffered(1))


def _gla_proj_kernel(x_ref, nw_ref, w_ref, wgu_ref, bg_ref, proj_ref, la_ref, *, tn):
    xn = _rms(x_ref[...], nw_ref[...]).astype(BF16)
    g_low = _dot(xn, w_ref[:, proj_ref.shape[1]:])
    z = _dot(g_low.astype(BF16), wgu_ref[...]) + bg_ref[...]
    la_ref[...] = -_softplus(-z) / GLA_TAU
    for c in range(proj_ref.shape[1] // tn):
        cols = slice(c * tn, (c + 1) * tn)
        proj_ref[:, cols] = _dot(xn, w_ref[:, cols]).astype(proj_ref.dtype)


def _gla_proj(x2, norm_w, w_in, w_gu, b_g, *, tm, tn):
    t, d = x2.shape
    rank, key = w_gu.shape
    n = w_in.shape[1] - rank
    return pl.pallas_call(
        functools.partial(_gla_proj_kernel, tn=tn),
        grid=(t // tm,),
        in_specs=[
            pl.BlockSpec((tm, d), lambda i: (i, 0)),
            _const_spec((1, d)),
            _const_spec((d, n + rank)),
            _const_spec((rank, key)),
            _const_spec((1, key)),
        ],
        out_specs=[
            pl.BlockSpec((tm, n), lambda i: (i, 0)),
            pl.BlockSpec((tm, key), lambda i: (i, 0)),
        ],
        out_shape=[
            jax.ShapeDtypeStruct((t, n), BF16),
            jax.ShapeDtypeStruct((t, key), F32),
        ],
        compiler_params=_params(("arbitrary",)),
        name="gla_proj",
    )(x2, norm_w, w_in, w_gu, b_g)


def _gla_scan_kernel(q_ref, k_ref, v_ref, la_ref, o_ref, st_ref, gc_ref, *,
                     nseq, heads, dk, dv, n_chunks):
    @pl.when(pl.program_id(1) == 0)
    def _():
        st_ref[...] = jnp.zeros_like(st_ref)

    causal, _ = _tri_masks(CHUNK)
    krow = lax.broadcasted_iota(jnp.int32, (CHUNK, dk), 0)
    scale = dk ** -0.5
    sh = [(s, h) for s in range(nseq) for h in range(heads)]
    n = range(len(sh))

    for s in range(nseq):
        for cc in range(n_chunks):
            crow = slice(cc * CHUNK, (cc + 1) * CHUNK)
            gc_ref[s, crow, :] = _chunk_cumsum(la_ref[s, crow, :])

    def chunk_body(c, carry):
        rows = pl.ds(pl.multiple_of(c * CHUNK, CHUNK), CHUNK)
        g = [gc_ref[s, rows, h * dk:(h + 1) * dk] for s, h in sh]
        q = [q_ref[s, rows, h * dk:(h + 1) * dk].astype(F32) * scale for s, h in sh]
        k = [k_ref[s, rows, h * dk:(h + 1) * dk].astype(F32) for s, h in sh]
        v = [v_ref[s, rows, h * dv:(h + 1) * dv] for s, h in sh]
        nsub = CHUNK // SUB
        qz = jnp.zeros((SUB, dk), BF16)
        q_rows = [[] for _ in n]
        k_slots = [[] for _ in n]
        for b in range(nsub):
            for i in n:
                ref_row = g[i][b * SUB:b * SUB + 1, :]
                qb = q[i][b * SUB:(b + 1) * SUB, :] * jnp.exp(g[i][b * SUB:(b + 1) * SUB, :] - ref_row)
                e = jnp.where(krow < (b + 1) * SUB, ref_row - g[i], 0.0)
                k_slots[i].append((k[i] * jnp.exp(e)).astype(BF16))
                q_rows[i].append(jnp.concatenate(
                    [qb.astype(BF16) if b2 == b else qz for b2 in range(nsub)], axis=1))
        scores = [jnp.where(causal, _dot_nt(jnp.concatenate(q_rows[i], axis=0),
                                            jnp.concatenate(k_slots[i], axis=1)), 0.0).astype(BF16)
                  for i in n]
        st = [st_ref[s, h] for s, h in sh]
        o = [_dot(scores[i], v[i]) + _dot_nt((q[i] * jnp.exp(g[i])).astype(BF16), st[i].astype(BF16))
             for i in n]
        g_last = [g[i][CHUNK - 1:CHUNK, :] for i in n]
        k_dec = [(k[i] * jnp.exp(g_last[i] - g[i])).astype(BF16) for i in n]
        for i, (s, h) in enumerate(sh):
            st_ref[s, h] = st[i] * jnp.exp(g_last[i]) + _dot_tn(v[i], k_dec[i])
            o_ref[s, rows, h * dv:(h + 1) * dv] = o[i].astype(o_ref.dtype)
        return carry

    lax.fori_loop(0, n_chunks, chunk_body, 0)


def _seq_groups(a, nseq):
    return a.reshape(nseq, a.shape[0] // nseq, a.shape[1])


def _gla_scan(proj, log_a, *, batch, seq, heads, dk, dv, lb, nseq):
    key = heads * dk
    val = heads * dv
    nb = seq // lb
    kern = functools.partial(_gla_scan_kernel, nseq=nseq, heads=heads, dk=dk, dv=dv,
                             n_chunks=lb // CHUNK)
    proj3 = _seq_groups(proj, nseq)
    spec = lambda width, col: pl.BlockSpec((nseq, lb, width), lambda b, l: (0, b * nb + l, col))
    out = pl.pallas_call(
        kern,
        grid=(batch // nseq, nb),
        in_specs=[
            spec(key, 0),
            spec(key, 1),
            spec(val, 2 * key // val),
            spec(key, 0),
        ],
        out_specs=spec(val, 0),
        out_shape=jax.ShapeDtypeStruct((nseq, batch * seq // nseq, val), BF16),
        scratch_shapes=[pltpu.VMEM((nseq, heads, dv, dk), F32), pltpu.VMEM((nseq, lb, key), F32)],
        compiler_params=_params(("arbitrary", "arbitrary")),
        name="gla_scan",
    )(proj3, proj3, proj3, _seq_groups(log_a, nseq))
    return out.reshape(batch * seq, val)


def _post_kernel(h_ref, o_ref, gate_ref, ow_ref, wout_ref, nw_ref, wgu_ref, wd_ref, fw_ref, out_ref, *,
                 d_ff, final):
    dv = ow_ref.shape[1]
    ow = ow_ref[...]
    gated = [(_rms(o_ref[:, lo:lo + dv].astype(F32), ow)
              * _silu(gate_ref[:, lo:lo + dv].astype(F32))).astype(BF16)
             for lo in range(0, o_ref.shape[1], dv)]
    h1 = h_ref[...] + _dot(jnp.concatenate(gated, axis=1), wout_ref[...])
    xn = _rms(h1, nw_ref[...]).astype(BF16)
    gu = _dot(xn, wgu_ref[...])
    act = (_silu(gu[:, :d_ff]) * gu[:, d_ff:]).astype(BF16)
    out = h1 + _dot(act, wd_ref[...])
    if final:
        out = _rms(out, fw_ref[...])
    out_ref[...] = out


def _post(h2, o2, proj, gate_block, o_norm_w, w_out, norm_w, w_gu, w_d, final_w, *, layer, tm, final):
    t, d = h2.shape
    val = o2.shape[1]
    d_ff = w_d.shape[1]
    layer_spec = lambda shape: pl.BlockSpec((None,) + shape, lambda i: (layer, 0, 0),
                                            pipeline_mode=pl.Buffered(1))
    return pl.pallas_call(
        functools.partial(_post_kernel, d_ff=d_ff, final=final),
        grid=(t // tm,),
        in_specs=[
            pl.BlockSpec((tm, d), lambda i: (i, 0)),
            pl.BlockSpec((tm, val), lambda i: (i, 0)),
            pl.BlockSpec((tm, val), lambda i: (i, gate_block)),
            _const_spec((1, o_norm_w.shape[1])),
            _const_spec((val, d)),
            _const_spec((1, d)),
            layer_spec((d, 2 * d_ff)),
            layer_spec((d_ff, d)),
            _const_spec((1, d)),
        ],
        out_specs=pl.BlockSpec((tm, d), lambda i: (i, 0)),
        out_shape=jax.ShapeDtypeStruct((t, d), F32),
        compiler_params=_params(("arbitrary",)),
        name="post_final" if final else "post",
    )(h2, o2, proj, o_norm_w, w_out, norm_w, w_gu, w_d, final_w)


def _spread_order(n_heavy, n_light):
    items = [((a + 0.5) / n_heavy, 0, a) for a in range(n_heavy)]
    items += [((b + 0.5) / n_light, 1, b) for b in range(n_light)]
    return [(kind, idx) for _, kind, idx in sorted(items)]


def _gdn_proj_kernel(x_ref, nw_ref, w_ref, wba_ref, alog_ref, dt_ref, cw_ref,
                     proj_ref, gb_ref, carry_ref, *bufs,
                     tiles_per_seq, key, conv_ch, heads, head_dim, tn):
    tm = x_ref.shape[0]
    ybufs = bufs[:PROJ_STAGES]
    obufs = bufs[PROJ_STAGES:]

    @pl.when(pl.program_id(0) % tiles_per_seq == 0)
    def _():
        carry_ref[...] = jnp.zeros_like(carry_ref)

    xn = _rms(x_ref[...], nw_ref[...]).astype(BF16)
    ba = _dot(xn, wba_ref[...])
    lane = lax.broadcasted_iota(jnp.int32, ba.shape, 1)
    g = -jnp.exp(alog_ref[...]) * _softplus(ba + dt_ref[...])
    gb_ref[...] = jnp.where(lane < heads, g, jnp.where(lane < 2 * heads, _sigmoid(ba), 0.0))

    def matmul(c, slot):
        y = _dot(xn, w_ref[:, c * tn:(c + 1) * tn])
        if c * tn >= conv_ch:
            proj_ref[:, c * tn:(c + 1) * tn] = y.astype(proj_ref.dtype)
            return
        for hh in range(tn // head_dim):
            ybufs[slot][hh, CARRY_ROWS:, :] = y[:, hh * head_dim:(hh + 1) * head_dim]

    def epilogue(c, slot):
        if c * tn >= conv_ch:
            return
        for hh in range(tn // head_dim):
            lo = c * tn + hh * head_dim
            ybufs[slot][hh, 0:CARRY_ROWS, :] = carry_ref[:, lo:lo + head_dim]
            carry_ref[:, lo:lo + head_dim] = ybufs[slot][hh, tm:tm + CARRY_ROWS, :]
            cw = 0.5 * cw_ref[:, lo:lo + head_dim]
            taps_w = [cw[GDN_CONV - 1 - s:GDN_CONV - s, :] for s in range(GDN_CONV)]
            for rb in range(tm // CONV_ROWS):
                r0 = rb * CONV_ROWS
                taps = {a: ybufs[slot][hh, pl.ds(CARRY_ROWS + r0 + a, SUBLANES, stride=CONV_STRIDE), :]
                        for a in range(1 - GDN_CONV, CONV_STRIDE)}
                for a in range(CONV_STRIDE):
                    half = taps[a] * taps_w[0]
                    for s in range(1, GDN_CONV):
                        half = half + taps[a - s] * taps_w[s]
                    act = half + half * jnp.tanh(half)
                    if lo < 2 * key:
                        inv = lax.rsqrt(jnp.sum(act * act, axis=-1, keepdims=True) + L2_EPS)
                        if lo < key:
                            inv = inv * head_dim ** -0.5
                        act = act * inv
                    obufs[slot][hh, pl.ds(r0 + a, SUBLANES, stride=CONV_STRIDE), :] = act
                proj_ref[r0:r0 + CONV_ROWS, lo:lo + head_dim] = (
                    obufs[slot][hh, r0:r0 + CONV_ROWS, :].astype(proj_ref.dtype))

    n_conv = conv_ch // tn
    n_plain = proj_ref.shape[1] // tn - n_conv
    order = [idx if kind == 0 else n_conv + idx for kind, idx in _spread_order(n_conv, n_plain)]
    ahead = PROJ_STAGES - 1
    for pos in range(min(ahead, len(order))):
        matmul(order[pos], pos % PROJ_STAGES)
    for pos, c in enumerate(order):
        if pos + ahead < len(order):
            matmul(order[pos + ahead], (pos + ahead) % PROJ_STAGES)
        epilogue(c, pos % PROJ_STAGES)


def _gdn_proj(x2, norm_w, w_in, w_ba, a_log, dt_bias, conv_w, *, seq, tm, tn, n, key, conv_ch,
              heads, head_dim):
    t, d = x2.shape
    kern = functools.partial(_gdn_proj_kernel, tiles_per_seq=seq // tm, key=key, conv_ch=conv_ch,
                             heads=heads, head_dim=head_dim, tn=tn)
    return pl.pallas_call(
        kern,
        grid=(t // tm,),
        in_specs=[
            pl.BlockSpec((tm, d), lambda i: (i, 0)),
            _const_spec((1, d)),
            _const_spec(w_in.shape),
            _const_spec((d, LANES)),
            _const_spec((1, LANES)),
            _const_spec((1, LANES)),
            _const_spec((GDN_CONV, conv_ch)),
        ],
        out_specs=[
            pl.BlockSpec((tm, n), lambda i: (i, 0)),
            pl.BlockSpec((tm, LANES), lambda i: (i, 0)),
        ],
        out_shape=[
            jax.ShapeDtypeStruct((t, n), BF16),
            jax.ShapeDtypeStruct((t, LANES), F32),
        ],
        scratch_shapes=(
            [pltpu.VMEM((CARRY_ROWS, conv_ch), F32)]
            + [pltpu.VMEM((tn // head_dim, tm + CARRY_ROWS, head_dim), F32)] * PROJ_STAGES
            + [pltpu.VMEM((tn // head_dim, tm, head_dim), F32)] * PROJ_STAGES),
        compiler_params=_params(("arbitrary",)),
        name="gdn_proj",
    )(x2, norm_w, w_in, w_ba, a_log, dt_bias, conv_w)


def _gdn_scan_kernel(q_ref, k_ref, v_ref, gb_ref, o_ref, s_ref, gc_ref, gr_ref, *,
                     nseq, qk_heads, rep, hd, n_chunks):
    v_heads = qk_heads * rep
    width = rep * CHUNK
    rr = range(rep)

    @pl.when(pl.program_id(1) == 0)
    def _():
        s_ref[...] = jnp.zeros_like(s_ref)

    row = lax.broadcasted_iota(jnp.int32, (CHUNK, width), 0)
    lane = lax.broadcasted_iota(jnp.int32, (CHUNK, width), 1)
    blk = lane // CHUNK
    col = lane - blk * CHUNK
    causal_p = row >= col
    strict_p = row > col
    eye_p = jnp.where(row == col, 1.0, 0.0)

    def per_head(cols):
        out = cols[rep - 1]
        for r in reversed(range(rep - 1)):
            out = jnp.where(blk == r, cols[r], out)
        return out

    def block_diag(x):
        return jnp.concatenate([jnp.where(blk == r, x, jnp.zeros_like(x)) for r in rr], axis=0)

    def block_rows(parts):
        return jnp.concatenate(
            [jnp.concatenate([parts[r] if r2 == r else jnp.zeros_like(parts[r2]) for r2 in rr], axis=1)
             for r in rr], axis=0)

    def stack_heads(x):
        return jnp.concatenate(
            [x if r == 0 else pltpu.roll(x, LANES - r * qk_heads, 1) for r in rr], axis=0)

    sp = [(s, p) for s in range(nseq) for p in range(qk_heads)]
    n = range(len(sp))

    def vcols(p, r):
        return slice((p * rep + r) * hd, (p * rep + r + 1) * hd)

    for s in range(nseq):
        for cc in range(n_chunks):
            crow = slice(cc * CHUNK, (cc + 1) * CHUNK)
            g_cum = _chunk_cumsum(gb_ref[s, crow, :])
            gc_ref[s, crow, :] = g_cum
            gr_ref[s, cc] = stack_heads(g_cum).T[:gr_ref.shape[2], :]

    def chunk_body(c, carry):
        rows = pl.ds(pl.multiple_of(c * CHUNK, CHUNK), CHUNK)
        gb = [gb_ref[s, rows, :] for s in range(nseq)]
        g_cum = [gc_ref[s, rows, :] for s in range(nseq)]
        g_rows = [gr_ref[s, c] for s in range(nseq)]
        k = [k_ref[s, rows, p * hd:(p + 1) * hd] for s, p in sp]
        q = [q_ref[s, rows, p * hd:(p + 1) * hd] for s, p in sp]
        kq = [_dot_nt(jnp.concatenate([k[i], q[i]], axis=0), jnp.concatenate([k[i]] * rep, axis=0))
              for i in n]
        g_c = [[jnp.broadcast_to(g_cum[s][:, r * qk_heads + p:r * qk_heads + p + 1], (CHUNK, hd))
                for r in rr] for s, p in sp]
        b_c = [[jnp.broadcast_to(gb[s][:, v_heads + r * qk_heads + p:v_heads + r * qk_heads + p + 1],
                                 (CHUNK, hd)) for r in rr] for s, p in sp]
        decay = [jnp.where(causal_p,
                           jnp.exp(jnp.where(causal_p, per_head(g_c[i]) - g_rows[s][p:p + 1, :], 0.0)), 0.0)
                 for i, (s, p) in enumerate(sp)]
        e_g = [[jnp.exp(g) for g in g_c[i]] for i in n]
        kf = [k[i].astype(F32) for i in n]
        qf = [q[i].astype(F32) for i in n]
        st = [s_ref[s, p] for s, p in sp]
        ks_qs = [[_dot(jnp.concatenate([kf[i] * (b_c[i][r] * e_g[i][r]), qf[i] * e_g[i][r]],
                                       axis=0).astype(BF16),
                       st[i][:, r * hd:(r + 1) * hd].astype(BF16)) for r in rr] for i in n]
        ks = [[ks_qs[i][r][:CHUNK] for r in rr] for i in n]
        qs = [[ks_qs[i][r][CHUNK:] for r in rr] for i in n]
        pm = [jnp.where(strict_p, -(kq[i][:CHUNK] * per_head(b_c[i]) * decay[i]), 0.0) for i in n]
        t = [eye_p + pm[i] for i in n]
        pmb = [x.astype(BF16) for x in pm]
        pm = [_dot(pmb[i], block_diag(pmb[i])) for i in n]
        span = 2
        while 2 * span < CHUNK:
            pmb = [x.astype(BF16) for x in pm]
            both = [_dot(pmb[i], jnp.concatenate([block_diag(t[i].astype(BF16)), block_diag(pmb[i])],
                                                 axis=1)) for i in n]
            t = [t[i] + both[i][:, :width] for i in n]
            pm = [both[i][:, width:] for i in n]
            span *= 2
        t = [(t[i] + _dot(pm[i].astype(BF16), block_diag(t[i].astype(BF16)))).astype(BF16) for i in n]
        x = [block_rows([(v_ref[s, rows, vcols(p, r)].astype(F32) * b_c[i][r]
                          - ks[i][r]).astype(BF16) for r in rr])
             for i, (s, p) in enumerate(sp)]
        v_new = [_dot(t[i], x[i]) for i in n]
        v_blk = [block_rows([v_new[i][:, r * hd:(r + 1) * hd].astype(BF16) for r in rr]) for i in n]
        attn = [jnp.where(causal_p, kq[i][CHUNK:] * decay[i], 0.0).astype(BF16) for i in n]
        o_intra = [_dot(attn[i], v_blk[i]) for i in n]
        g_last = [[g[CHUNK - 1:CHUNK, :] for g in g_c[i]] for i in n]
        k_dec = [jnp.concatenate([kf[i] * jnp.exp(g_last[i][r] - g_c[i][r]) for r in rr],
                                 axis=0).astype(BF16) for i in n]
        e_last = [jnp.concatenate([jnp.broadcast_to(jnp.exp(g), (1, hd)) for g in g_last[i]], axis=1)
                  for i in n]
        for i, (s, p) in enumerate(sp):
            s_ref[s, p] = st[i] * e_last[i] + _dot_tn(k_dec[i], v_blk[i])
            for r in rr:
                o = qs[i][r] + o_intra[i][:, r * hd:(r + 1) * hd]
                o_ref[s, rows, vcols(p, r)] = o.astype(o_ref.dtype)
        return carry

    lax.fori_loop(0, n_chunks, chunk_body, 0)


def _gdn_scan(proj, gb, *, batch, seq, qk_heads, rep, hd, lb, nseq):
    assert rep * CHUNK == LANES and hd == LANES
    key = qk_heads * hd
    val = key * rep
    nb = seq // lb
    kern = functools.partial(_gdn_scan_kernel, nseq=nseq, qk_heads=qk_heads, rep=rep, hd=hd,
                             n_chunks=lb // CHUNK)
    proj3 = _seq_groups(proj, nseq)
    spec = lambda width, col: pl.BlockSpec((nseq, lb, width), lambda b, l: (0, b * nb + l, col))
    out = pl.pallas_call(
        kern,
        grid=(batch // nseq, nb),
        in_specs=[
            spec(key, 0),
            spec(key, 1),
            spec(val, 2 * key // val),
            spec(LANES, 0),
        ],
        out_specs=spec(val, 0),
        out_shape=jax.ShapeDtypeStruct((nseq, batch * seq // nseq, val), BF16),
        scratch_shapes=[pltpu.VMEM((nseq, qk_heads, hd, rep * hd), F32),
                        pltpu.VMEM((nseq, lb, LANES), F32),
                        pltpu.VMEM((nseq, lb // CHUNK, pl.cdiv(qk_heads, SUBLANES) * SUBLANES, rep * CHUNK),
                                   F32)],
        compiler_params=_params(("arbitrary", "arbitrary")),
        name="gdn_scan",
    )(proj3, proj3, proj3, _seq_groups(gb, nseq))
    return out.reshape(batch * seq, val)


def _pad_lanes(a):
    return jnp.pad(a, ((0, 0), (0, LANES - a.shape[1])))


def kernel(x, gla_w_in, gla_w_gate_up, gla_b_gate, gla_norm_w, gla_w_out, gdn_w_in, gdn_conv_w,
           gdn_a_log, gdn_dt_bias, gdn_norm_w, gdn_w_out, mix_norm_w, ffn_norm_w, ffn_w_gate_up,
           ffn_w_down, final_norm_w):
    batch, seq, d = x.shape
    depth = mix_norm_w.shape[0]
    t = batch * seq

    gla_rank, gla_key = gla_w_gate_up.shape[1:]
    gla_dv = gla_norm_w.shape[1]
    gla_val = gla_w_out.shape[1]
    gla_heads = gla_val // gla_dv
    gla_dk = gla_key // gla_heads

    hd = gdn_norm_w.shape[1]
    gdn_v_heads = gdn_a_log.shape[1]
    gdn_val = gdn_w_out.shape[1]
    gdn_conv_ch = gdn_conv_w.shape[2]
    gdn_key = (gdn_conv_ch - gdn_val) // 2
    gdn_qk_heads = gdn_key // hd
    gdn_main = gdn_conv_ch + gdn_val

    tiles = _tiles(batch, seq)
    assert seq % tiles["lb"] == 0 and tiles["lb"] % CHUNK == 0
    assert seq % tiles["tm"] == 0 and seq % tiles["tm_gla_proj"] == 0 and tiles["tm"] % CONV_ROWS == 0
    assert gdn_main % tiles["tn_gdn_proj"] == 0 and gdn_conv_ch % tiles["tn_gdn_proj"] == 0
    assert (2 * gla_key + 2 * gla_val) % tiles["tn_gla_proj"] == 0
    row = lambda a: a.reshape(1, -1)
    ffn_gu = ffn_w_gate_up.astype(BF16)
    ffn_d = ffn_w_down.astype(BF16)

    h = x.reshape(t, d)
    for i in range(depth):
        j = i // 2
        if i % 2 == 0:
            proj, log_a = _gla_proj(
                h, row(mix_norm_w[i]), gla_w_in[j].astype(BF16), gla_w_gate_up[j].astype(BF16),
                row(gla_b_gate[j]), tm=tiles["tm_gla_proj"], tn=tiles["tn_gla_proj"])
            o = _gla_scan(proj, log_a, batch=batch, seq=seq, heads=gla_heads, dk=gla_dk, dv=gla_dv,
                          lb=tiles["lb"], nseq=tiles["nseq_gla"])
            w_out, o_norm_w = gla_w_out[j], gla_norm_w[j]
            gate_block = 2 * gla_key // gla_val + 1
        else:
            w_in = gdn_w_in[j]
            rep = gdn_v_heads // gdn_qk_heads
            perm = jnp.array([p * rep + r for r in range(rep) for p in range(gdn_qk_heads)])
            w_b = w_in[:, gdn_main:gdn_main + gdn_v_heads][:, perm]
            w_a = w_in[:, gdn_main + gdn_v_heads:][:, perm]
            proj, gb = _gdn_proj(
                h, row(mix_norm_w[i]), w_in.astype(BF16),
                _pad_lanes(jnp.concatenate([w_a, w_b], axis=1)).astype(BF16),
                _pad_lanes(row(gdn_a_log[j][perm])),
                _pad_lanes(row(gdn_dt_bias[j][perm])),
                gdn_conv_w[j], seq=seq, tm=tiles["tm"], tn=tiles["tn_gdn_proj"], n=gdn_main,
                key=gdn_key, conv_ch=gdn_conv_ch, heads=gdn_v_heads, head_dim=hd)
            o = _gdn_scan(proj, gb, batch=batch, seq=seq, qk_heads=gdn_qk_heads, rep=rep, hd=hd,
                          lb=tiles["lb"], nseq=tiles["nseq_gdn"])
            w_out, o_norm_w = gdn_w_out[j], gdn_norm_w[j]
            gate_block = 2 * gdn_key // gdn_val + 1
        h = _post(h, o, proj, gate_block, row(o_norm_w), w_out.astype(BF16), row(ffn_norm_w[i]),
                  ffn_gu, ffn_d, row(final_norm_w), layer=i, tm=tiles["tm"], final=(i == depth - 1))
    return h.reshape(batch, seq, d)
```

```python
import functools

import jax
import jax.numpy as jnp
from jax import lax
from jax.experimental import pallas as pl
from jax.experimental.pallas import tpu as pltpu

F32 = jnp.float32
BF16 = jnp.bfloat16

NORM_EPS = 1e-6
L2_EPS = 1e-6
LANES = 128
SUBLANES = 8
MXU_COLS = 256
VMEM_BYTES = 64 * 1024 * 1024
VMEM_LIMIT = VMEM_BYTES * 7 // 8
CHUNK = 64
SUB = 16
GLA_TAU = 16.0
GDN_CONV = 4
CARRY_ROWS = SUBLANES
CONV_STRIDE = 4
CONV_ROWS = SUBLANES * CONV_STRIDE
PROJ_STAGES = 2


def _tiles(batch, seq):
    return dict(
        tm_gla_proj=min(1024, seq),
        tm=min(512, seq),
        tm_gdn_proj=min(256, seq),
        tn_gla_proj=4 * MXU_COLS,
        tn_gdn_proj=MXU_COLS,
        lb=min(512, seq),
        nseq_gla=next(n for n in (4, 2, 1) if batch % n == 0),
        nseq_gdn=next(n for n in (2, 1) if batch % n == 0),
    )


def _dot(a, b):
    return jnp.dot(a, b, preferred_element_type=F32)


def _dot_nt(a, b):
    return lax.dot_general(a, b, (((1,), (1,)), ((), ())), preferred_element_type=F32)


def _dot_tn(a, b):
    return lax.dot_general(a, b, (((0,), (0,)), ((), ())), preferred_element_type=F32)


def _chunk_cumsum(x):
    hi = x.astype(BF16)
    r1 = x - hi.astype(F32)
    mid = r1.astype(BF16)
    lo = (r1 - mid.astype(F32)).astype(BF16)
    row = lax.broadcasted_iota(jnp.int32, (CHUNK, 3 * CHUNK), 0)
    col = lax.broadcasted_iota(jnp.int32, (CHUNK, 3 * CHUNK), 1)
    tril3 = jnp.where(row >= col % CHUNK, 1.0, 0.0).astype(BF16)
    return _dot(tril3, jnp.concatenate([hi, mid, lo], axis=0))


def _rms(x, w):
    return x * lax.rsqrt(jnp.mean(x * x, axis=-1, keepdims=True) + NORM_EPS) * w


def _sigmoid(x):
    return 1.0 / (1.0 + jnp.exp(-x))


def _silu(x):
    half = 0.5 * x
    return half + half * jnp.tanh(half)


def _softplus(x):
    return jnp.maximum(x, 0.0) + jnp.log(1.0 + jnp.exp(-jnp.abs(x)))


def _tri_masks(n):
    row = lax.broadcasted_iota(jnp.int32, (n, n), 0)
    col = lax.broadcasted_iota(jnp.int32, (n, n), 1)
    return row >= col, row > col


def _params(sem):
    return pltpu.CompilerParams(dimension_semantics=sem, vmem_limit_bytes=VMEM_LIMIT)


def _const_spec(shape):
    return pl.BlockSpec(shape, lambda *_: (0,) * len(shape), pipeline_mode=pl.Buffered(1))


def _gla_proj_kernel(x_ref, nw_ref, w_ref, wgu_ref, bg_ref, proj_ref, la_ref, *, tn):
    xn = _rms(x_ref[...], nw_ref[...]).astype(BF16)
    g_low = _dot(xn, w_ref[:, proj_ref.shape[1]:])
    z = _dot(g_low.astype(BF16), wgu_ref[...]) + bg_ref[...]
    la_ref[...] = -_softplus(-z) / GLA_TAU
    for c in range(proj_ref.shape[1] // tn):
        cols = slice(c * tn, (c + 1) * tn)
        proj_ref[:, cols] = _dot(xn, w_ref[:, cols]).astype(proj_ref.dtype)


def _gla_proj(x2, norm_w, w_in, w_gu, b_g, *, tm, tn):
    t, d = x2.shape
    rank, key = w_gu.shape
    n = w_in.shape[1] - rank
    return pl.pallas_call(
        functools.partial(_gla_proj_kernel, tn=tn),
        grid=(t // tm,),
        in_specs=[
            pl.BlockSpec((tm, d), lambda i: (i, 0)),
            _const_spec((1, d)),
            _const_spec((d, n + rank)),
            _const_spec((rank, key)),
            _const_spec((1, key)),
        ],
        out_specs=[
            pl.BlockSpec((tm, n), lambda i: (i, 0)),
            pl.BlockSpec((tm, key), lambda i: (i, 0)),
        ],
        out_shape=[
            jax.ShapeDtypeStruct((t, n), BF16),
            jax.ShapeDtypeStruct((t, key), F32),
        ],
        compiler_params=_params(("arbitrary",)),
        name="gla_proj",
    )(x2, norm_w, w_in, w_gu, b_g)


def _gla_scan_kernel(q_ref, k_ref, v_ref, la_ref, o_ref, st_ref, gc_ref, *,
                     nseq, heads, dk, dv, n_chunks):
    @pl.when(pl.program_id(1) == 0)
    def _():
        st_ref[...] = jnp.zeros_like(st_ref)

    causal, _ = _tri_masks(CHUNK)
    krow = lax.broadcasted_iota(jnp.int32, (CHUNK, dk), 0)
    scale = dk ** -0.5
    sh = [(s, h) for s in range(nseq) for h in range(heads)]
    n = range(len(sh))

    for s in range(nseq):
        for cc in range(n_chunks):
            crow = slice(cc * CHUNK, (cc + 1) * CHUNK)
            gc_ref[s, crow, :] = _chunk_cumsum(la_ref[s, crow, :])

    def chunk_body(c, carry):
        rows = pl.ds(pl.multiple_of(c * CHUNK, CHUNK), CHUNK)
        g = [gc_ref[s, rows, h * dk:(h + 1) * dk] for s, h in sh]
        q = [q_ref[s, rows, h * dk:(h + 1) * dk].astype(F32) * scale for s, h in sh]
        k = [k_ref[s, rows, h * dk:(h + 1) * dk].astype(F32) for s, h in sh]
        v = [v_ref[s, rows, h * dv:(h + 1) * dv] for s, h in sh]
        nsub = CHUNK // SUB
        qz = jnp.zeros((SUB, dk), BF16)
        q_rows = [[] for _ in n]
        k_slots = [[] for _ in n]
        for b in range(nsub):
            for i in n:
                ref_row = g[i][b * SUB:b * SUB + 1, :]
                qb = q[i][b * SUB:(b + 1) * SUB, :] * jnp.exp(g[i][b * SUB:(b + 1) * SUB, :] - ref_row)
                e = jnp.where(krow < (b + 1) * SUB, ref_row - g[i], 0.0)
                k_slots[i].append((k[i] * jnp.exp(e)).astype(BF16))
                q_rows[i].append(jnp.concatenate(
                    [qb.astype(BF16) if b2 == b else qz for b2 in range(nsub)], axis=1))
        scores = [jnp.where(causal, _dot_nt(jnp.concatenate(q_rows[i], axis=0),
                                            jnp.concatenate(k_slots[i], axis=1)), 0.0).astype(BF16)
                  for i in n]
        st = [st_ref[s, h] for s, h in sh]
        o = [_dot(scores[i], v[i]) + _dot_nt((q[i] * jnp.exp(g[i])).astype(BF16), st[i].astype(BF16))
             for i in n]
        g_last = [g[i][CHUNK - 1:CHUNK, :] for i in n]
        k_dec = [(k[i] * jnp.exp(g_last[i] - g[i])).astype(BF16) for i in n]
        for i, (s, h) in enumerate(sh):
            st_ref[s, h] = st[i] * jnp.exp(g_last[i]) + _dot_tn(v[i], k_dec[i])
            o_ref[s, rows, h * dv:(h + 1) * dv] = o[i].astype(o_ref.dtype)
        return carry

    lax.fori_loop(0, n_chunks, chunk_body, 0)


def _seq_groups(a, nseq):
    return a.reshape(nseq, a.shape[0] // nseq, a.shape[1])


def _gla_scan(proj, log_a, *, batch, seq, heads, dk, dv, lb, nseq):
    key = heads * dk
    val = heads * dv
    nb = seq // lb
    kern = functools.partial(_gla_scan_kernel, nseq=nseq, heads=heads, dk=dk, dv=dv,
                             n_chunks=lb // CHUNK)
    proj3 = _seq_groups(proj, nseq)
    spec = lambda width, col: pl.BlockSpec((nseq, lb, width), lambda b, l: (0, b * nb + l, col))
    out = pl.pallas_call(
        kern,
        grid=(batch // nseq, nb),
        in_specs=[
            spec(key, 0),
            spec(key, 1),
            spec(val, 2 * key // val),
            spec(key, 0),
        ],
        out_specs=spec(val, 0),
        out_shape=jax.ShapeDtypeStruct((nseq, batch * seq // nseq, val), BF16),
        scratch_shapes=[pltpu.VMEM((nseq, heads, dv, dk), F32), pltpu.VMEM((nseq, lb, key), F32)],
        compiler_params=_params(("arbitrary", "arbitrary")),
        name="gla_scan",
    )(proj3, proj3, proj3, _seq_groups(log_a, nseq))
    return out.reshape(batch * seq, val)


def _post_kernel(h_ref, o_ref, gate_ref, ow_ref, wout_ref, nw_ref, wgu_ref, wd_ref, fw_ref, out_ref, *,
                 d_ff, final):
    dv = ow_ref.shape[1]
    ow = ow_ref[...]
    gated = [(_rms(o_ref[:, lo:lo + dv].astype(F32), ow)
              * _silu(gate_ref[:, lo:lo + dv].astype(F32))).astype(BF16)
             for lo in range(0, o_ref.shape[1], dv)]
    h1 = h_ref[...] + _dot(jnp.concatenate(gated, axis=1), wout_ref[...])
    xn = _rms(h1, nw_ref[...]).astype(BF16)
    gu = _dot(xn, wgu_ref[...])
    act = (_silu(gu[:, :d_ff]) * gu[:, d_ff:]).astype(BF16)
    out = h1 + _dot(act, wd_ref[...])
    if final:
        out = _rms(out, fw_ref[...])
    out_ref[...] = out


def _post(h2, o2, proj, gate_block, o_norm_w, w_out, norm_w, w_gu, w_d, final_w, *, layer, tm, final):
    t, d = h2.shape
    val = o2.shape[1]
    d_ff = w_d.shape[1]
    layer_spec = lambda shape: pl.BlockSpec((None,) + shape, lambda i: (layer, 0, 0),
                                            pipeline_mode=pl.Buffered(1))
    return pl.pallas_call(
        functools.partial(_post_kernel, d_ff=d_ff, final=final),
        grid=(t // tm,),
        in_specs=[
            pl.BlockSpec((tm, d), lambda i: (i, 0)),
            pl.BlockSpec((tm, val), lambda i: (i, 0)),
            pl.BlockSpec((tm, val), lambda i: (i, gate_block)),
            _const_spec((1, o_norm_w.shape[1])),
            _const_spec((val, d)),
            _const_spec((1, d)),
            layer_spec((d, 2 * d_ff)),
            layer_spec((d_ff, d)),
            _const_spec((1, d)),
        ],
        out_specs=pl.BlockSpec((tm, d), lambda i: (i, 0)),
        out_shape=jax.ShapeDtypeStruct((t, d), F32),
        compiler_params=_params(("arbitrary",)),
        name="post_final" if final else "post",
    )(h2, o2, proj, o_norm_w, w_out, norm_w, w_gu, w_d, final_w)


def _spread_order(n_heavy, n_light):
    items = [((a + 0.5) / n_heavy, 0, a) for a in range(n_heavy)]
    items += [((b + 0.5) / n_light, 1, b) for b in range(n_light)]
    return [(kind, idx) for _, kind, idx in sorted(items)]


def _gdn_proj_kernel(x_ref, nw_ref, w_ref, wba_ref, alog_ref, dt_ref, cw_ref,
                     proj_ref, gb_ref, carry_ref, *bufs,
                     tiles_per_seq, key, conv_ch, heads, head_dim, tn):
    tm = x_ref.shape[0]
    ybufs = bufs[:PROJ_STAGES]
    obufs = bufs[PROJ_STAGES:]

    @pl.when(pl.program_id(0) % tiles_per_seq == 0)
    def _():
        carry_ref[...] = jnp.zeros_like(carry_ref)

    xn = _rms(x_ref[...], nw_ref[...]).astype(BF16)
    ba = _dot(xn, wba_ref[...])
    lane = lax.broadcasted_iota(jnp.int32, ba.shape, 1)
    g = -jnp.exp(alog_ref[...]) * _softplus(ba + dt_ref[...])
    gb_ref[...] = jnp.where(lane < heads, g, jnp.where(lane < 2 * heads, _sigmoid(ba), 0.0))

    def matmul(c, slot):
        y = _dot(xn, w_ref[:, c * tn:(c + 1) * tn])
        if c * tn >= conv_ch:
            proj_ref[:, c * tn:(c + 1) * tn] = y.astype(proj_ref.dtype)
            return
        for hh in range(tn // head_dim):
            ybufs[slot][hh, CARRY_ROWS:, :] = y[:, hh * head_dim:(hh + 1) * head_dim]

    def epilogue(c, slot):
        if c * tn >= conv_ch:
            return
        for hh in range(tn // head_dim):
            lo = c * tn + hh * head_dim
            ybufs[slot][hh, 0:CARRY_ROWS, :] = carry_ref[:, lo:lo + head_dim]
            carry_ref[:, lo:lo + head_dim] = ybufs[slot][hh, tm:tm + CARRY_ROWS, :]
            cw = 0.5 * cw_ref[:, lo:lo + head_dim]
            taps_w = [cw[GDN_CONV - 1 - s:GDN_CONV - s, :] for s in range(GDN_CONV)]
            for rb in range(tm // CONV_ROWS):
                r0 = rb * CONV_ROWS
                taps = {a: ybufs[slot][hh, pl.ds(CARRY_ROWS + r0 + a, SUBLANES, stride=CONV_STRIDE), :]
                        for a in range(1 - GDN_CONV, CONV_STRIDE)}
                for a in range(CONV_STRIDE):
                    half = taps[a] * taps_w[0]
                    for s in range(1, GDN_CONV):
                        half = half + taps[a - s] * taps_w[s]
                    act = half + half * jnp.tanh(half)
                    if lo < 2 * key:
                        inv = lax.rsqrt(jnp.sum(act * act, axis=-1, keepdims=True) + L2_EPS)
                        if lo < key:
                            inv = inv * head_dim ** -0.5
                        act = act * inv
                    obufs[slot][hh, pl.ds(r0 + a, SUBLANES, stride=CONV_STRIDE), :] = act
                proj_ref[r0:r0 + CONV_ROWS, lo:lo + head_dim] = (
                    obufs[slot][hh, r0:r0 + CONV_ROWS, :].astype(proj_ref.dtype))

    n_conv = conv_ch // tn
    n_plain = proj_ref.shape[1] // tn - n_conv
    order = [idx if kind == 0 else n_conv + idx for kind, idx in _spread_order(n_conv, n_plain)]
    ahead = PROJ_STAGES - 1
    for pos in range(min(ahead, len(order))):
        matmul(order[pos], pos % PROJ_STAGES)
    for pos, c in enumerate(order):
        if pos + ahead < len(order):
            matmul(order[pos + ahead], (pos + ahead) % PROJ_STAGES)
        epilogue(c, pos % PROJ_STAGES)


def _gdn_proj(x2, norm_w, w_in, w_ba, a_log, dt_bias, conv_w, *, seq, tm, tn, n, key, conv_ch,
              heads, head_dim):
    t, d = x2.shape
    kern = functools.partial(_gdn_proj_kernel, tiles_per_seq=seq // tm, key=key, conv_ch=conv_ch,
                             heads=heads, head_dim=head_dim, tn=tn)
    return pl.pallas_call(
        kern,
        grid=(t // tm,),
        in_specs=[
            pl.BlockSpec((tm, d), lambda i: (i, 0)),
            _const_spec((1, d)),
            _const_spec(w_in.shape),
            _const_spec((d, LANES)),
            _const_spec((1, LANES)),
            _const_spec((1, LANES)),
            _const_spec((GDN_CONV, conv_ch)),
        ],
        out_specs=[
            pl.BlockSpec((tm, n), lambda i: (i, 0)),
            pl.BlockSpec((tm, LANES), lambda i: (i, 0)),
        ],
        out_shape=[
            jax.ShapeDtypeStruct((t, n), BF16),
            jax.ShapeDtypeStruct((t, LANES), F32),
        ],
        scratch_shapes=(
            [pltpu.VMEM((CARRY_ROWS, conv_ch), F32)]
            + [pltpu.VMEM((tn // head_dim, tm + CARRY_ROWS, head_dim), F32)] * PROJ_STAGES
            + [pltpu.VMEM((tn // head_dim, tm, head_dim), F32)] * PROJ_STAGES),
        compiler_params=_params(("arbitrary",)),
        name="gdn_proj",
    )(x2, norm_w, w_in, w_ba, a_log, dt_bias, conv_w)


def _gdn_scan_kernel(q_ref, k_ref, v_ref, gb_ref, o_ref, s_ref, gc_ref, gr_ref, *,
                     nseq, qk_heads, rep, hd, n_chunks):
    v_heads = qk_heads * rep
    width = rep * CHUNK
    rr = range(rep)

    @pl.when(pl.program_id(1) == 0)
    def _():
        s_ref[...] = jnp.zeros_like(s_ref)

    row = lax.broadcasted_iota(jnp.int32, (CHUNK, width), 0)
    lane = lax.broadcasted_iota(jnp.int32, (CHUNK, width), 1)
    blk = lane // CHUNK
    col = lane - blk * CHUNK
    causal_p = row >= col
    strict_p = row > col
    eye_p = jnp.where(row == col, 1.0, 0.0)

    def per_head(cols):
        out = cols[rep - 1]
        for r in reversed(range(rep - 1)):
            out = jnp.where(blk == r, cols[r], out)
        return out

    def block_diag(x):
        return jnp.concatenate([jnp.where(blk == r, x, jnp.zeros_like(x)) for r in rr], axis=0)

    def block_rows(parts):
        return jnp.concatenate(
            [jnp.concatenate([parts[r] if r2 == r else jnp.zeros_like(parts[r2]) for r2 in rr], axis=1)
             for r in rr], axis=0)

    def stack_heads(x):
        return jnp.concatenate(
            [x if r == 0 else pltpu.roll(x, LANES - r * qk_heads, 1) for r in rr], axis=0)

    sp = [(s, p) for s in range(nseq) for p in range(qk_heads)]
    n = range(len(sp))

    def vcols(p, r):
        return slice((p * rep + r) * hd, (p * rep + r + 1) * hd)

    for s in range(nseq):
        for cc in range(n_chunks):
            crow = slice(cc * CHUNK, (cc + 1) * CHUNK)
            g_cum = _chunk_cumsum(gb_ref[s, crow, :])
            gc_ref[s, crow, :] = g_cum
            gr_ref[s, cc] = stack_heads(g_cum).T[:gr_ref.shape[2], :]

    def chunk_body(c, carry):
        rows = pl.ds(pl.multiple_of(c * CHUNK, CHUNK), CHUNK)
        gb = [gb_ref[s, rows, :] for s in range(nseq)]
        g_cum = [gc_ref[s, rows, :] for s in range(nseq)]
        g_rows = [gr_ref[s, c] for s in range(nseq)]
        k = [k_ref[s, rows, p * hd:(p + 1) * hd] for s, p in sp]
        q = [q_ref[s, rows, p * hd:(p + 1) * hd] for s, p in sp]
        kq = [_dot_nt(jnp.concatenate([k[i], q[i]], axis=0), jnp.concatenate([k[i]] * rep, axis=0))
              for i in n]
        g_c = [[jnp.broadcast_to(g_cum[s][:, r * qk_heads + p:r * qk_heads + p + 1], (CHUNK, hd))
                for r in rr] for s, p in sp]
        b_c = [[jnp.broadcast_to(gb[s][:, v_heads + r * qk_heads + p:v_heads + r * qk_heads + p + 1],
                                 (CHUNK, hd)) for r in rr] for s, p in sp]
        decay = [jnp.where(causal_p,
                           jnp.exp(jnp.where(causal_p, per_head(g_c[i]) - g_rows[s][p:p + 1, :], 0.0)), 0.0)
                 for i, (s, p) in enumerate(sp)]
        e_g = [[jnp.exp(g) for g in g_c[i]] for i in n]
        kf = [k[i].astype(F32) for i in n]
        qf = [q[i].astype(F32) for i in n]
        st = [s_ref[s, p] for s, p in sp]
        ks_qs = [[_dot(jnp.concatenate([kf[i] * (b_c[i][r] * e_g[i][r]), qf[i] * e_g[i][r]],
                                       axis=0).astype(BF16),
                       st[i][:, r * hd:(r + 1) * hd].astype(BF16)) for r in rr] for i in n]
        ks = [[ks_qs[i][r][:CHUNK] for r in rr] for i in n]
        qs = [[ks_qs[i][r][CHUNK:] for r in rr] for i in n]
        pm = [jnp.where(strict_p, -(kq[i][:CHUNK] * per_head(b_c[i]) * decay[i]), 0.0) for i in n]
        t = [eye_p + pm[i] for i in n]
        pmb = [x.astype(BF16) for x in pm]
        pm = [_dot(pmb[i], block_diag(pmb[i])) for i in n]
        span = 2
        while 2 * span < CHUNK:
            pmb = [x.astype(BF16) for x in pm]
            both = [_dot(pmb[i], jnp.concatenate([block_diag(t[i].astype(BF16)), block_diag(pmb[i])],
                                                 axis=1)) for i in n]
            t = [t[i] + both[i][:, :width] for i in n]
            pm = [both[i][:, width:] for i in n]
            span *= 2
        t = [(t[i] + _dot(pm[i].astype(BF16), block_diag(t[i].astype(BF16)))).astype(BF16) for i in n]
        x = [block_rows([(v_ref[s, rows, vcols(p, r)].astype(F32) * b_c[i][r]
                          - ks[i][r]).astype(BF16) for r in rr])
             for i, (s, p) in enumerate(sp)]
        v_new = [_dot(t[i], x[i]) for i in n]
        v_blk = [block_rows([v_new[i][:, r * hd:(r + 1) * hd].astype(BF16) for r in rr]) for i in n]
        attn = [jnp.where(causal_p, kq[i][CHUNK:] * decay[i], 0.0).astype(BF16) for i in n]
        o_intra = [_dot(attn[i], v_blk[i]) for i in n]
        g_last = [[g[CHUNK - 1:CHUNK, :] for g in g_c[i]] for i in n]
        k_dec = [jnp.concatenate([kf[i] * jnp.exp(g_last[i][r] - g_c[i][r]) for r in rr],
                                 axis=0).astype(BF16) for i in n]
        e_last = [jnp.concatenate([jnp.broadcast_to(jnp.exp(g), (1, hd)) for g in g_last[i]], axis=1)
                  for i in n]
        for i, (s, p) in enumerate(sp):
            s_ref[s, p] = st[i] * e_last[i] + _dot_tn(k_dec[i], v_blk[i])
            for r in rr:
                o = qs[i][r] + o_intra[i][:, r * hd:(r + 1) * hd]
                o_ref[s, rows, vcols(p, r)] = o.astype(o_ref.dtype)
        return carry

    lax.fori_loop(0, n_chunks, chunk_body, 0)


def _gdn_scan(proj, gb, *, batch, seq, qk_heads, rep, hd, lb, nseq):
    assert rep * CHUNK == LANES and hd == LANES
    key = qk_heads * hd
    val = key * rep
    nb = seq // lb
    kern = functools.partial(_gdn_scan_kernel, nseq=nseq, qk_heads=qk_heads, rep=rep, hd=hd,
                             n_chunks=lb // CHUNK)
    proj3 = _seq_groups(proj, nseq)
    spec = lambda width, col: pl.BlockSpec((nseq, lb, width), lambda b, l: (0, b * nb + l, col))
    out = pl.pallas_call(
        kern,
        grid=(batch // nseq, nb),
        in_specs=[
            spec(key, 0),
            spec(key, 1),
            spec(val, 2 * key // val),
            spec(LANES, 0),
        ],
        out_specs=spec(val, 0),
        out_shape=jax.ShapeDtypeStruct((nseq, batch * seq // nseq, val), BF16),
        scratch_shapes=[pltpu.VMEM((nseq, qk_heads, hd, rep * hd), F32),
                        pltpu.VMEM((nseq, lb, LANES), F32),
                        pltpu.VMEM((nseq, lb // CHUNK, pl.cdiv(qk_heads, SUBLANES) * SUBLANES, rep * CHUNK),
                                   F32)],
        compiler_params=_params(("arbitrary", "arbitrary")),
        name="gdn_scan",
    )(proj3, proj3, proj3, _seq_groups(gb, nseq))
    return out.reshape(batch * seq, val)


def _pad_lanes(a):
    return jnp.pad(a, ((0, 0), (0, LANES - a.shape[1])))


def kernel(x, gla_w_in, gla_w_gate_up, gla_b_gate, gla_norm_w, gla_w_out, gdn_w_in, gdn_conv_w,
           gdn_a_log, gdn_dt_bias, gdn_norm_w, gdn_w_out, mix_norm_w, ffn_norm_w, ffn_w_gate_up,
           ffn_w_down, final_norm_w):
    batch, seq, d = x.shape
    depth = mix_norm_w.shape[0]
    t = batch * seq

    gla_rank, gla_key = gla_w_gate_up.shape[1:]
    gla_dv = gla_norm_w.shape[1]
    gla_val = gla_w_out.shape[1]
    gla_heads = gla_val // gla_dv
    gla_dk = gla_key // gla_heads

    hd = gdn_norm_w.shape[1]
    gdn_v_heads = gdn_a_log.shape[1]
    gdn_val = gdn_w_out.shape[1]
    gdn_conv_ch = gdn_conv_w.shape[2]
    gdn_key = (gdn_conv_ch - gdn_val) // 2
    gdn_qk_heads = gdn_key // hd
    gdn_main = gdn_conv_ch + gdn_val

    tiles = _tiles(batch, seq)
    assert seq % tiles["lb"] == 0 and tiles["lb"] % CHUNK == 0
    assert seq % tiles["tm"] == 0 and seq % tiles["tm_gla_proj"] == 0
    assert seq % tiles["tm_gdn_proj"] == 0 and tiles["tm_gdn_proj"] % CONV_ROWS == 0
    assert gdn_main % tiles["tn_gdn_proj"] == 0 and gdn_conv_ch % tiles["tn_gdn_proj"] == 0
    assert (2 * gla_key + 2 * gla_val) % tiles["tn_gla_proj"] == 0
    row = lambda a: a.reshape(1, -1)
    ffn_gu = ffn_w_gate_up.astype(BF16)
    ffn_d = ffn_w_down.astype(BF16)

    h = x.reshape(t, d)
    for i in range(depth):
        j = i // 2
        if i % 2 == 0:
            proj, log_a = _gla_proj(
                h, row(mix_norm_w[i]), gla_w_in[j].astype(BF16), gla_w_gate_up[j].astype(BF16),
                row(gla_b_gate[j]), tm=tiles["tm_gla_proj"], tn=tiles["tn_gla_proj"])
            o = _gla_scan(proj, log_a, batch=batch, seq=seq, heads=gla_heads, dk=gla_dk, dv=gla_dv,
                          lb=tiles["lb"], nseq=tiles["nseq_gla"])
            w_out, o_norm_w = gla_w_out[j], gla_norm_w[j]
            gate_block = 2 * gla_key // gla_val + 1
        else:
            w_in = gdn_w_in[j]
            rep = gdn_v_heads // gdn_qk_heads
            perm = jnp.array([p * rep + r for r in range(rep) for p in range(gdn_qk_heads)])
            w_b = w_in[:, gdn_main:gdn_main + gdn_v_heads][:, perm]
            w_a = w_in[:, gdn_main + gdn_v_heads:][:, perm]
            proj, gb = _gdn_proj(
                h, row(mix_norm_w[i]), w_in.astype(BF16),
                _pad_lanes(jnp.concatenate([w_a, w_b], axis=1)).astype(BF16),
                _pad_lanes(row(gdn_a_log[j][perm])),
                _pad_lanes(row(gdn_dt_bias[j][perm])),
                gdn_conv_w[j], seq=seq, tm=tiles["tm_gdn_proj"], tn=tiles["tn_gdn_proj"], n=gdn_main,
                key=gdn_key, conv_ch=gdn_conv_ch, heads=gdn_v_heads, head_dim=hd)
            o = _gdn_scan(proj, gb, batch=batch, seq=seq, qk_heads=gdn_qk_heads, rep=rep, hd=hd,
                          lb=tiles["lb"], nseq=tiles["nseq_gdn"])
            w_out, o_norm_w = gdn_w_out[j], gdn_norm_w[j]
            gate_block = 2 * gdn_key // gdn_val + 1
        h = _post(h, o, proj, gate_block, row(o_norm_w), w_out.astype(BF16), row(ffn_norm_w[i]),
                  ffn_gu, ffn_d, row(final_norm_w), layer=i, tm=tiles["tm"], final=(i == depth - 1))
    return h.reshape(batch, seq, d)
```

```python
import functools

import jax
import jax.numpy as jnp
from jax import lax
from jax.experimental import pallas as pl
from jax.experimental.pallas import tpu as pltpu

F32 = jnp.float32
BF16 = jnp.bfloat16

LN2 = 0.6931471805599453
NORM_EPS = 1e-6
L2_EPS = 1e-6
LANES = 128
SUBLANES = 8
MXU_COLS = 256
VMEM_BYTES = 64 * 1024 * 1024
VMEM_LIMIT = VMEM_BYTES * 7 // 8
CHUNK = 64
SUB = 16
GLA_TAU = 16.0
GDN_CONV = 4
CARRY_ROWS = SUBLANES
CONV_STRIDE = 4
CONV_ROWS = SUBLANES * CONV_STRIDE
PROJ_STAGES = 2


def _tiles(batch, seq):
    return dict(
        tm_gla_proj=min(1024, seq),
        tm=min(512, seq),
        tm_gdn_proj=min(256, seq),
        tn_gla_proj=2 * MXU_COLS,
        tn_gdn_proj=MXU_COLS,
        lb_gla=min(512, seq),
        lb_gdn=min(256, seq),
        nseq_gla=next(n for n in (4, 2, 1) if batch % n == 0),
        nseq_gdn=next(n for n in (4, 2, 1) if batch % n == 0),
    )


def _dot(a, b):
    return jnp.dot(a, b, preferred_element_type=F32)


def _dot_nt(a, b):
    return lax.dot_general(a, b, (((1,), (1,)), ((), ())), preferred_element_type=F32)


def _dot_tn(a, b):
    return lax.dot_general(a, b, (((0,), (0,)), ((), ())), preferred_element_type=F32)


def _chunk_cumsum(x):
    hi = x.astype(BF16)
    r1 = x - hi.astype(F32)
    mid = r1.astype(BF16)
    lo = (r1 - mid.astype(F32)).astype(BF16)
    row = lax.broadcasted_iota(jnp.int32, (CHUNK, 3 * CHUNK), 0)
    col = lax.broadcasted_iota(jnp.int32, (CHUNK, 3 * CHUNK), 1)
    tril3 = jnp.where(row >= col % CHUNK, 1.0, 0.0).astype(BF16)
    return _dot(tril3, jnp.concatenate([hi, mid, lo], axis=0))


def _rms(x, w):
    return x * lax.rsqrt(jnp.mean(x * x, axis=-1, keepdims=True) + NORM_EPS) * w


def _sigmoid(x):
    return 1.0 / (1.0 + jnp.exp(-x))


def _silu(x):
    half = 0.5 * x
    return half + half * jnp.tanh(half)


def _softplus(x):
    return jnp.maximum(x, 0.0) + jnp.log(1.0 + jnp.exp(-jnp.abs(x)))


def _tri_masks(n):
    row = lax.broadcasted_iota(jnp.int32, (n, n), 0)
    col = lax.broadcasted_iota(jnp.int32, (n, n), 1)
    return row >= col, row > col


def _params(sem):
    return pltpu.CompilerParams(dimension_semantics=sem, vmem_limit_bytes=VMEM_LIMIT)


def _const_spec(shape):
    return pl.BlockSpec(shape, lambda *_: (0,) * len(shape), pipeline_mode=pl.Buffered(1))


def _gla_proj_kernel(x_ref, nw_ref, w_ref, wgu_ref, bg_ref, proj_ref, la_ref, *, tn):
    xn = _rms(x_ref[...], nw_ref[...]).astype(BF16)
    g_low = _dot(xn, w_ref[:, proj_ref.shape[1]:])
    z = _dot(g_low.astype(BF16), wgu_ref[...]) + bg_ref[...]
    la_ref[...] = (jnp.minimum(z, 0.0) * (1.0 / GLA_TAU)
                   - (LN2 / GLA_TAU) * jnp.log2(1.0 + jnp.exp2(jnp.abs(z) * (-1.0 / LN2))))
    for c in range(proj_ref.shape[1] // tn):
        cols = slice(c * tn, (c + 1) * tn)
        proj_ref[:, cols] = _dot(xn, w_ref[:, cols]).astype(proj_ref.dtype)


def _gla_proj(x2, norm_w, w_in, w_gu, b_g, *, tm, tn):
    t, d = x2.shape
    rank, key = w_gu.shape
    n = w_in.shape[1] - rank
    return pl.pallas_call(
        functools.partial(_gla_proj_kernel, tn=tn),
        grid=(t // tm,),
        in_specs=[
            pl.BlockSpec((tm, d), lambda i: (i, 0)),
            _const_spec((1, d)),
            _const_spec((d, n + rank)),
            _const_spec((rank, key)),
            _const_spec((1, key)),
        ],
        out_specs=[
            pl.BlockSpec((tm, n), lambda i: (i, 0)),
            pl.BlockSpec((tm, key), lambda i: (i, 0)),
        ],
        out_shape=[
            jax.ShapeDtypeStruct((t, n), BF16),
            jax.ShapeDtypeStruct((t, key), F32),
        ],
        compiler_params=_params(("arbitrary",)),
        name="gla_proj",
    )(x2, norm_w, w_in, w_gu, b_g)


def _gla_scan_kernel(q_ref, k_ref, v_ref, la_ref, o_ref, st_ref, gc_ref, *,
                     nseq, heads, dk, dv, n_chunks):
    @pl.when(pl.program_id(1) == 0)
    def _():
        st_ref[...] = jnp.zeros_like(st_ref)

    causal, _ = _tri_masks(CHUNK)
    krow = lax.broadcasted_iota(jnp.int32, (CHUNK, dk), 0)
    scale = dk ** -0.5
    sh = [(s, h) for s in range(nseq) for h in range(heads)]
    n = range(len(sh))

    for s in range(nseq):
        for cc in range(n_chunks):
            crow = slice(cc * CHUNK, (cc + 1) * CHUNK)
            gc_ref[s, crow, :] = _chunk_cumsum(la_ref[s, crow, :])

    def chunk_body(c, carry):
        rows = pl.ds(pl.multiple_of(c * CHUNK, CHUNK), CHUNK)
        g = [gc_ref[s, rows, h * dk:(h + 1) * dk] for s, h in sh]
        q = [q_ref[s, rows, h * dk:(h + 1) * dk].astype(F32) * scale for s, h in sh]
        k = [k_ref[s, rows, h * dk:(h + 1) * dk].astype(F32) for s, h in sh]
        v = [v_ref[s, rows, h * dv:(h + 1) * dv] for s, h in sh]
        nsub = CHUNK // SUB
        qz = jnp.zeros((SUB, dk), BF16)
        q_rows = [[] for _ in n]
        k_slots = [[] for _ in n]
        for b in range(nsub):
            for i in n:
                ref_row = g[i][b * SUB:b * SUB + 1, :]
                qb = q[i][b * SUB:(b + 1) * SUB, :] * jnp.exp(g[i][b * SUB:(b + 1) * SUB, :] - ref_row)
                e = jnp.where(krow < (b + 1) * SUB, ref_row - g[i], 0.0)
                k_slots[i].append((k[i] * jnp.exp(e)).astype(BF16))
                q_rows[i].append(jnp.concatenate(
                    [qb.astype(BF16) if b2 == b else qz for b2 in range(nsub)], axis=1))
        scores = [jnp.where(causal, _dot_nt(jnp.concatenate(q_rows[i], axis=0),
                                            jnp.concatenate(k_slots[i], axis=1)), 0.0).astype(BF16)
                  for i in n]
        st = [st_ref[s, h] for s, h in sh]
        o = [_dot(scores[i], v[i]) + _dot_nt((q[i] * jnp.exp(g[i])).astype(BF16), st[i].astype(BF16))
             for i in n]
        g_last = [g[i][CHUNK - 1:CHUNK, :] for i in n]
        k_dec = [(k[i] * jnp.exp(g_last[i] - g[i])).astype(BF16) for i in n]
        for i, (s, h) in enumerate(sh):
            st_ref[s, h] = st[i] * jnp.exp(g_last[i]) + _dot_tn(v[i], k_dec[i])
            o_ref[s, rows, h * dv:(h + 1) * dv] = o[i].astype(o_ref.dtype)
        return carry

    lax.fori_loop(0, n_chunks, chunk_body, 0)


def _seq_groups(a, nseq):
    return a.reshape(nseq, a.shape[0] // nseq, a.shape[1])


def _gla_scan(proj, log_a, *, batch, seq, heads, dk, dv, lb, nseq):
    key = heads * dk
    val = heads * dv
    nb = seq // lb
    kern = functools.partial(_gla_scan_kernel, nseq=nseq, heads=heads, dk=dk, dv=dv,
                             n_chunks=lb // CHUNK)
    proj3 = _seq_groups(proj, nseq)
    spec = lambda width, col: pl.BlockSpec((nseq, lb, width), lambda b, l: (0, b * nb + l, col))
    out = pl.pallas_call(
        kern,
        grid=(batch // nseq, nb),
        in_specs=[
            spec(key, 0),
            spec(key, 1),
            spec(val, 2 * key // val),
            spec(key, 0),
        ],
        out_specs=spec(val, 0),
        out_shape=jax.ShapeDtypeStruct((nseq, batch * seq // nseq, val), BF16),
        scratch_shapes=[pltpu.VMEM((nseq, heads, dv, dk), F32), pltpu.VMEM((nseq, lb, key), F32)],
        compiler_params=_params(("arbitrary", "arbitrary")),
        name="gla_scan",
    )(proj3, proj3, proj3, _seq_groups(log_a, nseq))
    return out.reshape(batch * seq, val)


def _post_kernel(h_ref, o_ref, gate_ref, ow_ref, wout_ref, nw_ref, wgu_ref, wd_ref, fw_ref, out_ref, *,
                 d_ff, final):
    dv = ow_ref.shape[1]
    ow = ow_ref[...]
    gated = [(_rms(o_ref[:, lo:lo + dv].astype(F32), ow)
              * _silu(gate_ref[:, lo:lo + dv].astype(F32))).astype(BF16)
             for lo in range(0, o_ref.shape[1], dv)]
    h1 = h_ref[...] + _dot(jnp.concatenate(gated, axis=1), wout_ref[...])
    xn = _rms(h1, nw_ref[...]).astype(BF16)
    gu = _dot(xn, wgu_ref[...])
    act = (_silu(gu[:, :d_ff]) * gu[:, d_ff:]).astype(BF16)
    out = h1 + _dot(act, wd_ref[...])
    if final:
        out = _rms(out, fw_ref[...])
    out_ref[...] = out


def _post(h2, o2, proj, gate_block, o_norm_w, w_out, norm_w, w_gu, w_d, final_w, *, layer, tm, final):
    t, d = h2.shape
    val = o2.shape[1]
    d_ff = w_d.shape[1]
    layer_spec = lambda shape: pl.BlockSpec((None,) + shape, lambda i: (layer, 0, 0),
                                            pipeline_mode=pl.Buffered(1))
    return pl.pallas_call(
        functools.partial(_post_kernel, d_ff=d_ff, final=final),
        grid=(t // tm,),
        in_specs=[
            pl.BlockSpec((tm, d), lambda i: (i, 0)),
            pl.BlockSpec((tm, val), lambda i: (i, 0)),
            pl.BlockSpec((tm, val), lambda i: (i, gate_block)),
            _const_spec((1, o_norm_w.shape[1])),
            _const_spec((val, d)),
            _const_spec((1, d)),
            layer_spec((d, 2 * d_ff)),
            layer_spec((d_ff, d)),
            _const_spec((1, d)),
        ],
        out_specs=pl.BlockSpec((tm, d), lambda i: (i, 0)),
        out_shape=jax.ShapeDtypeStruct((t, d), F32),
        compiler_params=_params(("arbitrary",)),
        name="post_final" if final else "post",
    )(h2, o2, proj, o_norm_w, w_out, norm_w, w_gu, w_d, final_w)


def _spread_order(n_heavy, n_light):
    items = [((a + 0.5) / n_heavy, 0, a) for a in range(n_heavy)]
    items += [((b + 0.5) / n_light, 1, b) for b in range(n_light)]
    return [(kind, idx) for _, kind, idx in sorted(items)]


def _gdn_proj_kernel(x_ref, nw_ref, w_ref, wba_ref, alog_ref, dt_ref, cw_ref,
                     proj_ref, gb_ref, carry_ref, *bufs,
                     tiles_per_seq, key, conv_ch, heads, head_dim, tn):
    tm = x_ref.shape[0]
    ybufs = bufs[:PROJ_STAGES]
    obufs = bufs[PROJ_STAGES:]

    @pl.when(pl.program_id(0) % tiles_per_seq == 0)
    def _():
        carry_ref[...] = jnp.zeros_like(carry_ref)

    xn = _rms(x_ref[...], nw_ref[...]).astype(BF16)
    ba = _dot(xn, wba_ref[...])
    lane = lax.broadcasted_iota(jnp.int32, ba.shape, 1)
    g = -jnp.exp(alog_ref[...]) * _softplus(ba + dt_ref[...])
    gb_ref[...] = jnp.where(lane < heads, g, jnp.where(lane < 2 * heads, _sigmoid(ba), 0.0))

    def matmul(c, slot):
        y = _dot(xn, w_ref[:, c * tn:(c + 1) * tn])
        if c * tn >= conv_ch:
            proj_ref[:, c * tn:(c + 1) * tn] = y.astype(proj_ref.dtype)
            return
        for hh in range(tn // head_dim):
            ybufs[slot][hh, CARRY_ROWS:, :] = y[:, hh * head_dim:(hh + 1) * head_dim]

    def epilogue(c, slot):
        if c * tn >= conv_ch:
            return
        for hh in range(tn // head_dim):
            lo = c * tn + hh * head_dim
            ybufs[slot][hh, 0:CARRY_ROWS, :] = carry_ref[:, lo:lo + head_dim]
            carry_ref[:, lo:lo + head_dim] = ybufs[slot][hh, tm:tm + CARRY_ROWS, :]
            cw = 0.5 * cw_ref[:, lo:lo + head_dim]
            taps_w = [cw[GDN_CONV - 1 - s:GDN_CONV - s, :] for s in range(GDN_CONV)]
            for rb in range(tm // CONV_ROWS):
                r0 = rb * CONV_ROWS
                taps = {a: ybufs[slot][hh, pl.ds(CARRY_ROWS + r0 + a, SUBLANES, stride=CONV_STRIDE), :]
                        for a in range(1 - GDN_CONV, CONV_STRIDE)}
                for a in range(CONV_STRIDE):
                    half = taps[a] * taps_w[0]
                    for s in range(1, GDN_CONV):
                        half = half + taps[a - s] * taps_w[s]
                    act = half + half * jnp.tanh(half)
                    if lo < 2 * key:
                        inv = lax.rsqrt(jnp.sum(act * act, axis=-1, keepdims=True) + L2_EPS)
                        if lo < key:
                            inv = inv * head_dim ** -0.5
                        act = act * inv
                    obufs[slot][hh, pl.ds(r0 + a, SUBLANES, stride=CONV_STRIDE), :] = act
                proj_ref[r0:r0 + CONV_ROWS, lo:lo + head_dim] = (
                    obufs[slot][hh, r0:r0 + CONV_ROWS, :].astype(proj_ref.dtype))

    n_conv = conv_ch // tn
    n_plain = proj_ref.shape[1] // tn - n_conv
    order = [idx if kind == 0 else n_conv + idx for kind, idx in _spread_order(n_conv, n_plain)]
    ahead = PROJ_STAGES - 1
    for pos in range(min(ahead, len(order))):
        matmul(order[pos], pos % PROJ_STAGES)
    for pos, c in enumerate(order):
        if pos + ahead < len(order):
            matmul(order[pos + ahead], (pos + ahead) % PROJ_STAGES)
        epilogue(c, pos % PROJ_STAGES)


def _gdn_proj(x2, norm_w, w_in, w_ba, a_log, dt_bias, conv_w, *, seq, tm, tn, n, key, conv_ch,
              heads, head_dim):
    t, d = x2.shape
    kern = functools.partial(_gdn_proj_kernel, tiles_per_seq=seq // tm, key=key, conv_ch=conv_ch,
                             heads=heads, head_dim=head_dim, tn=tn)
    return pl.pallas_call(
        kern,
        grid=(t // tm,),
        in_specs=[
            pl.BlockSpec((tm, d), lambda i: (i, 0)),
            _const_spec((1, d)),
            _const_spec(w_in.shape),
            _const_spec((d, LANES)),
            _const_spec((1, LANES)),
            _const_spec((1, LANES)),
            _const_spec((GDN_CONV, conv_ch)),
        ],
        out_specs=[
            pl.BlockSpec((tm, n), lambda i: (i, 0)),
            pl.BlockSpec((tm, LANES), lambda i: (i, 0)),
        ],
        out_shape=[
            jax.ShapeDtypeStruct((t, n), BF16),
            jax.ShapeDtypeStruct((t, LANES), F32),
        ],
        scratch_shapes=(
            [pltpu.VMEM((CARRY_ROWS, conv_ch), F32)]
            + [pltpu.VMEM((tn // head_dim, tm + CARRY_ROWS, head_dim), F32)] * PROJ_STAGES
            + [pltpu.VMEM((tn // head_dim, tm, head_dim), F32)] * PROJ_STAGES),
        compiler_params=_params(("arbitrary",)),
        name="gdn_proj",
    )(x2, norm_w, w_in, w_ba, a_log, dt_bias, conv_w)


def _gdn_scan_kernel(q_ref, k_ref, v_ref, gb_ref, o_ref, s_ref, gc_ref, gr_ref, *,
                     nseq, qk_heads, rep, hd, n_chunks):
    v_heads = qk_heads * rep
    width = rep * CHUNK
    rr = range(rep)

    @pl.when(pl.program_id(1) == 0)
    def _():
        s_ref[...] = jnp.zeros_like(s_ref)

    row = lax.broadcasted_iota(jnp.int32, (CHUNK, width), 0)
    lane = lax.broadcasted_iota(jnp.int32, (CHUNK, width), 1)
    blk = lane // CHUNK
    col = lane - blk * CHUNK
    causal_p = row >= col
    strict_p = row > col
    eye_p = jnp.where(row == col, 1.0, 0.0)

    def per_head(cols):
        out = cols[rep - 1]
        for r in reversed(range(rep - 1)):
            out = jnp.where(blk == r, cols[r], out)
        return out

    def block_diag(x):
        return jnp.concatenate([jnp.where(blk == r, x, jnp.zeros_like(x)) for r in rr], axis=0)

    def block_rows(parts):
        return jnp.concatenate(
            [jnp.concatenate([parts[r] if r2 == r else jnp.zeros_like(parts[r2]) for r2 in rr], axis=1)
             for r in rr], axis=0)

    def stack_heads(x):
        return jnp.concatenate(
            [x if r == 0 else pltpu.roll(x, LANES - r * qk_heads, 1) for r in rr], axis=0)

    sp = [(s, p) for s in range(nseq) for p in range(qk_heads)]
    n = range(len(sp))

    def vcols(p, r):
        return slice((p * rep + r) * hd, (p * rep + r + 1) * hd)

    for s in range(nseq):
        for cc in range(n_chunks):
            crow = slice(cc * CHUNK, (cc + 1) * CHUNK)
            g_cum = _chunk_cumsum(gb_ref[s, crow, :])
            gc_ref[s, crow, :] = g_cum
            gr_ref[s, cc] = stack_heads(g_cum).T[:gr_ref.shape[2], :]

    def chunk_body(c, carry):
        rows = pl.ds(pl.multiple_of(c * CHUNK, CHUNK), CHUNK)
        gb = [gb_ref[s, rows, :] for s in range(nseq)]
        g_cum = [gc_ref[s, rows, :] for s in range(nseq)]
        g_rows = [gr_ref[s, c] for s in range(nseq)]
        k = [k_ref[s, rows, p * hd:(p + 1) * hd] for s, p in sp]
        q = [q_ref[s, rows, p * hd:(p + 1) * hd] for s, p in sp]
        kq = [_dot_nt(jnp.concatenate([k[i], q[i]], axis=0), jnp.concatenate([k[i]] * rep, axis=0))
              for i in n]
        g_c = [[jnp.broadcast_to(g_cum[s][:, r * qk_heads + p:r * qk_heads + p + 1], (CHUNK, hd))
                for r in rr] for s, p in sp]
        b_c = [[jnp.broadcast_to(gb[s][:, v_heads + r * qk_heads + p:v_heads + r * qk_heads + p + 1],
                                 (CHUNK, hd)) for r in rr] for s, p in sp]
        decay = [jnp.where(causal_p,
                           jnp.exp(jnp.where(causal_p, per_head(g_c[i]) - g_rows[s][p:p + 1, :], 0.0)), 0.0)
                 for i, (s, p) in enumerate(sp)]
        e_g = [[jnp.exp(g) for g in g_c[i]] for i in n]
        kf = [k[i].astype(F32) for i in n]
        qf = [q[i].astype(F32) for i in n]
        st = [s_ref[s, p] for s, p in sp]
        ks_qs = [[_dot(jnp.concatenate([kf[i] * (b_c[i][r] * e_g[i][r]), qf[i] * e_g[i][r]],
                                       axis=0).astype(BF16),
                       st[i][:, r * hd:(r + 1) * hd].astype(BF16)) for r in rr] for i in n]
        ks = [[ks_qs[i][r][:CHUNK] for r in rr] for i in n]
        qs = [[ks_qs[i][r][CHUNK:] for r in rr] for i in n]
        pm = [jnp.where(strict_p, -(kq[i][:CHUNK] * per_head(b_c[i]) * decay[i]), 0.0) for i in n]
        t = [eye_p + pm[i] for i in n]
        pmb = [x.astype(BF16) for x in pm]
        pm = [_dot(pmb[i], block_diag(pmb[i])) for i in n]
        span = 2
        while 2 * span < CHUNK:
            pmb = [x.astype(BF16) for x in pm]
            both = [_dot(pmb[i], jnp.concatenate([block_diag(t[i].astype(BF16)), block_diag(pmb[i])],
                                                 axis=1)) for i in n]
            t = [t[i] + both[i][:, :width] for i in n]
            pm = [both[i][:, width:] for i in n]
            span *= 2
        t = [(t[i] + _dot(pm[i].astype(BF16), block_diag(t[i].astype(BF16)))).astype(BF16) for i in n]
        x = [block_rows([(v_ref[s, rows, vcols(p, r)].astype(F32) * b_c[i][r]
                          - ks[i][r]).astype(BF16) for r in rr])
             for i, (s, p) in enumerate(sp)]
        v_new = [_dot(t[i], x[i]) for i in n]
        v_blk = [block_rows([v_new[i][:, r * hd:(r + 1) * hd].astype(BF16) for r in rr]) for i in n]
        attn = [jnp.where(causal_p, kq[i][CHUNK:] * decay[i], 0.0).astype(BF16) for i in n]
        o_intra = [_dot(attn[i], v_blk[i]) for i in n]
        g_last = [[g[CHUNK - 1:CHUNK, :] for g in g_c[i]] for i in n]
        k_dec = [jnp.concatenate([kf[i] * jnp.exp(g_last[i][r] - g_c[i][r]) for r in rr],
                                 axis=0).astype(BF16) for i in n]
        e_last = [jnp.concatenate([jnp.broadcast_to(jnp.exp(g), (1, hd)) for g in g_last[i]], axis=1)
                  for i in n]
        for i, (s, p) in enumerate(sp):
            s_ref[s, p] = st[i] * e_last[i] + _dot_tn(k_dec[i], v_blk[i])
            for r in rr:
                o = qs[i][r] + o_intra[i][:, r * hd:(r + 1) * hd]
                o_ref[s, rows, vcols(p, r)] = o.astype(o_ref.dtype)
        return carry

    lax.fori_loop(0, n_chunks, chunk_body, 0)


def _gdn_scan(proj, gb, *, batch, seq, qk_heads, rep, hd, lb, nseq):
    assert rep * CHUNK == LANES and hd == LANES
    key = qk_heads * hd
    val = key * rep
    nb = seq // lb
    kern = functools.partial(_gdn_scan_kernel, nseq=nseq, qk_heads=qk_heads, rep=rep, hd=hd,
                             n_chunks=lb // CHUNK)
    proj3 = _seq_groups(proj, nseq)
    spec = lambda width, col: pl.BlockSpec((nseq, lb, width), lambda b, l: (0, b * nb + l, col))
    out = pl.pallas_call(
        kern,
        grid=(batch // nseq, nb),
        in_specs=[
            spec(key, 0),
            spec(key, 1),
            spec(val, 2 * key // val),
            spec(LANES, 0),
        ],
        out_specs=spec(val, 0),
        out_shape=jax.ShapeDtypeStruct((nseq, batch * seq // nseq, val), BF16),
        scratch_shapes=[pltpu.VMEM((nseq, qk_heads, hd, rep * hd), F32),
                        pltpu.VMEM((nseq, lb, LANES), F32),
                        pltpu.VMEM((nseq, lb // CHUNK, pl.cdiv(qk_heads, SUBLANES) * SUBLANES, rep * CHUNK),
                                   F32)],
        compiler_params=_params(("arbitrary", "arbitrary")),
        name="gdn_scan",
    )(proj3, proj3, proj3, _seq_groups(gb, nseq))
    return out.reshape(batch * seq, val)


def _pad_lanes(a):
    return jnp.pad(a, ((0, 0), (0, LANES - a.shape[1])))


def kernel(x, gla_w_in, gla_w_gate_up, gla_b_gate, gla_norm_w, gla_w_out, gdn_w_in, gdn_conv_w,
           gdn_a_log, gdn_dt_bias, gdn_norm_w, gdn_w_out, mix_norm_w, ffn_norm_w, ffn_w_gate_up,
           ffn_w_down, final_norm_w):
    batch, seq, d = x.shape
    depth = mix_norm_w.shape[0]
    t = batch * seq

    gla_rank, gla_key = gla_w_gate_up.shape[1:]
    gla_dv = gla_norm_w.shape[1]
    gla_val = gla_w_out.shape[1]
    gla_heads = gla_val // gla_dv
    gla_dk = gla_key // gla_heads

    hd = gdn_norm_w.shape[1]
    gdn_v_heads = gdn_a_log.shape[1]
    gdn_val = gdn_w_out.shape[1]
    gdn_conv_ch = gdn_conv_w.shape[2]
    gdn_key = (gdn_conv_ch - gdn_val) // 2
    gdn_qk_heads = gdn_key // hd
    gdn_main = gdn_conv_ch + gdn_val

    tiles = _tiles(batch, seq)
    assert all(seq % tiles[k] == 0 and tiles[k] % CHUNK == 0 for k in ("lb_gla", "lb_gdn"))
    assert seq % tiles["tm"] == 0 and seq % tiles["tm_gla_proj"] == 0
    assert seq % tiles["tm_gdn_proj"] == 0 and tiles["tm_gdn_proj"] % CONV_ROWS == 0
    assert gdn_main % tiles["tn_gdn_proj"] == 0 and gdn_conv_ch % tiles["tn_gdn_proj"] == 0
    assert (2 * gla_key + 2 * gla_val) % tiles["tn_gla_proj"] == 0
    row = lambda a: a.reshape(1, -1)
    ffn_gu = ffn_w_gate_up.astype(BF16)
    ffn_d = ffn_w_down.astype(BF16)

    h = x.reshape(t, d)
    for i in range(depth):
        j = i // 2
        if i % 2 == 0:
            proj, log_a = _gla_proj(
                h, row(mix_norm_w[i]), gla_w_in[j].astype(BF16), gla_w_gate_up[j].astype(BF16),
                row(gla_b_gate[j]), tm=tiles["tm_gla_proj"], tn=tiles["tn_gla_proj"])
            o = _gla_scan(proj, log_a, batch=batch, seq=seq, heads=gla_heads, dk=gla_dk, dv=gla_dv,
                          lb=tiles["lb_gla"], nseq=tiles["nseq_gla"])
            w_out, o_norm_w = gla_w_out[j], gla_norm_w[j]
            gate_block = 2 * gla_key // gla_val + 1
        else:
            w_in = gdn_w_in[j]
            rep = gdn_v_heads // gdn_qk_heads
            perm = jnp.array([p * rep + r for r in range(rep) for p in range(gdn_qk_heads)])
            w_b = w_in[:, gdn_main:gdn_main + gdn_v_heads][:, perm]
            w_a = w_in[:, gdn_main + gdn_v_heads:][:, perm]
            proj, gb = _gdn_proj(
                h, row(mix_norm_w[i]), w_in.astype(BF16),
                _pad_lanes(jnp.concatenate([w_a, w_b], axis=1)).astype(BF16),
                _pad_lanes(row(gdn_a_log[j][perm])),
                _pad_lanes(row(gdn_dt_bias[j][perm])),
                gdn_conv_w[j], seq=seq, tm=tiles["tm_gdn_proj"], tn=tiles["tn_gdn_proj"], n=gdn_main,
                key=gdn_key, conv_ch=gdn_conv_ch, heads=gdn_v_heads, head_dim=hd)
            o = _gdn_scan(proj, gb, batch=batch, seq=seq, qk_heads=gdn_qk_heads, rep=rep, hd=hd,
                          lb=tiles["lb_gdn"], nseq=tiles["nseq_gdn"])
            w_out, o_norm_w = gdn_w_out[j], gdn_norm_w[j]
            gate_block = 2 * gdn_key // gdn_val + 1
        h = _post(h, o, proj, gate_block, row(o_norm_w), w_out.astype(BF16), row(ffn_norm_w[i]),
                  ffn_gu, ffn_d, row(final_norm_w), layer=i, tm=tiles["tm"], final=(i == depth - 1))
    return h.reshape(batch, seq, d)
```
